```python
import math
import jax
import jax.numpy as jnp
from jax import lax
import numpy as np

D_MODEL = 1024
BATCH = 4
SEQ = 4096
DEPTH = 4

GRID_W = 64
CTX_LEN = 256
N_EVEN = (DEPTH + 1) // 2
N_ODD = DEPTH // 2
N_MOD = 6
RMS_EPS = 1e-6
ROPE_BASE = 10000.0
Q_BLOCK = 128

S5_WIDTH = D_MODEL // 2
S5_GROUP = 16
S5_GROUPS = S5_WIDTH // S5_GROUP
S5_STATE = 64
S5_DT_MIN = 1e-3
S5_DT_MAX = 1e-1

MLA_V = 64
MLA_HEADS = (D_MODEL - S5_WIDTH) // MLA_V
MLA_NOPE = 64
MLA_ROPE = 32
MLA_QK = MLA_NOPE + MLA_ROPE
MLA_Q_RANK = D_MODEL // 4
MLA_KV_RANK = D_MODEL // 8
EVEN_IN = S5_WIDTH + MLA_Q_RANK + MLA_KV_RANK + MLA_ROPE
EVEN_MIX = S5_WIDTH + MLA_HEADS * MLA_V

GQA_DIM = 128
GQA_HEADS = D_MODEL // GQA_DIM
GQA_KV_HEADS = GQA_HEADS // 2
GQA_GROUP = GQA_HEADS // GQA_KV_HEADS
ODD_Q = GQA_HEADS * GQA_DIM
ODD_KV = GQA_KV_HEADS * GQA_DIM
ODD_IN = ODD_Q + 2 * ODD_KV
ODD_MIX = GQA_HEADS * GQA_DIM

MOE_GROUPS = 4
MOE_PER_GROUP = 8
MOE_EXPERTS = MOE_GROUPS * MOE_PER_GROUP
MOE_TOP_K = 2
MOE_FF = D_MODEL // 2
MOE_BLOCK = 128

kernel_name = 'hybrid_s5_mla_gqa_hmoe_dit'


def rms_norm(x, g):
    xf = x.astype(jnp.float32)
    y = xf * lax.rsqrt(jnp.mean(xf * xf, axis=-1, keepdims=True) + RMS_EPS)
    return y.astype(x.dtype) * g


def ada_chunks(sc, w, b):
    return jnp.split(sc @ w + b, N_MOD, axis=-1)


def modulate(h, shift, scale):
    return h * (1 + scale) + shift


def rope_tables(row, col, d_rot):
    n_freq = d_rot // 4
    inv = ROPE_BASE ** (-jnp.arange(n_freq, dtype=jnp.float32) / n_freq)
    ang = jnp.stack([row[:, None] * inv, col[:, None] * inv], axis=1)
    return jnp.cos(ang)[:, None], jnp.sin(ang)[:, None]


def apply_rope(x, cos, sin):
    xr = x.reshape(*x.shape[:-1], 2, 2, x.shape[-1] // 4)
    x1, x2 = xr[..., 0, :], xr[..., 1, :]
    cos, sin = cos.astype(x.dtype), sin.astype(x.dtype)
    out = jnp.stack([x1 * cos - x2 * sin, x2 * cos + x1 * sin], axis=-2)
    return out.reshape(x.shape)


def attend_dense(q, k, v, scale):
    s = jnp.einsum('bqkgd,bskd->bkgqs', q, k, preferred_element_type=jnp.float32) * scale
    p = jax.nn.softmax(s, axis=-1).astype(v.dtype)
    return jnp.einsum('bkgqs,bskd->bqkgd', p, v)


def attend_latent(q, k, v, scale):
    b, lq = q.shape[:2]
    nb = lq // Q_BLOCK
    qb = jnp.moveaxis(q.reshape(b, nb, Q_BLOCK, *q.shape[2:]), 1, 0)
    ob = lax.map(lambda qi: attend_dense(qi, k, v, scale), qb)
    return jnp.moveaxis(ob, 0, 1).reshape(b, lq, *ob.shape[3:])


def _cmul(ar, ai, br, bi):
    return ar * br - ai * bi, ar * bi + ai * br


def _s5_combine(e1, e2):
    a1r, a1i, b1r, b1i = e1
    a2r, a2i, b2r, b2i = e2
    ar, ai = _cmul(a2r, a2i, a1r, a1i)
    br, bi = _cmul(a2r, a2i, b1r, b1i)
    return ar, ai, br + b2r, bi + b2i


def s5_direction(u_ctx, u_lat, a_re, a_im, log_dt, b_re, b_im, c_re, c_im, reverse, need_ctx):
    dt = jnp.exp(log_dt)[:, None]
    mag = jnp.exp(dt * a_re)
    th = dt * a_im
    ab_r, ab_i = mag * jnp.cos(th), mag * jnp.sin(th)
    den = a_re * a_re + a_im * a_im
    nr = ab_r - 1
    z_r = (nr * a_re + ab_i * a_im) / den
    z_i = (ab_i * a_re - nr * a_im) / den
    bb_r, bb_i = _cmul(z_r[..., None], z_i[..., None], b_re, b_im)

    def drive(u):
        return (jnp.einsum('blgp,gnp->blgn', u, bb_r), jnp.einsum('blgp,gnp->blgn', u, bb_i))

    def scan(bu_r, bu_i):
        shp = bu_r.shape
        _, _, h_r, h_i = lax.associative_scan(
            _s5_combine, (jnp.broadcast_to(ab_r, shp), jnp.broadcast_to(ab_i, shp), bu_r, bu_i),
            axis=1, reverse=reverse)
        return h_r, h_i

    def readout(h_r, h_i):
        return jnp.einsum('blgn,gpn->blgp', h_r, c_re) - jnp.einsum('blgn,gpn->blgp', h_i, c_im)

    hc_r, hc_i = scan(*drive(u_ctx))
    end = 0 if reverse else -1
    first = -1 if reverse else 0
    carry_r, carry_i = _cmul(ab_r, ab_i, hc_r[:, end], hc_i[:, end])
    bu_r, bu_i = drive(u_lat)
    bu_r = bu_r.at[:, first].add(carry_r)
    bu_i = bu_i.at[:, first].add(carry_i)
    y_lat = readout(*scan(bu_r, bu_i))
    y_ctx = readout(hc_r, hc_i) if need_ctx else None
    return y_lat, y_ctx


def s5_glu(y, u, d_skip, w_glu):
    g = jax.nn.gelu(y + d_skip * u)
    return g * jax.nn.sigmoid(g @ w_glu)


def s5_mixer(u_lat, u_ctx, p, need_ctx):
    b, l, _ = u_lat.shape
    ug_lat = u_lat.reshape(b, l, S5_GROUPS, S5_GROUP)
    ug_ctx = u_ctx.reshape(b, u_ctx.shape[1], S5_GROUPS, S5_GROUP)
    outs = [s5_direction(ug_ctx, ug_lat, p['a_re'][dd], p['a_im'][dd], p['log_dt'][dd],
                         p['b_re'][dd], p['b_im'][dd], p['c_re'][dd], p['c_im'][dd], rev, need_ctx)
            for dd, rev in enumerate((False, True))]
    y_lat = (outs[0][0] + outs[1][0]).reshape(u_lat.shape)
    out_lat = s5_glu(y_lat, u_lat, p['d'], p['w_glu'])
    out_ctx = None
    if need_ctx:
        y_ctx = (outs[0][1] + outs[1][1]).reshape(u_ctx.shape)
        out_ctx = s5_glu(y_ctx, u_ctx, p['d'], p['w_glu'])
    return out_lat, out_ctx


def mla_qkv(cq, ckv, kr, p, rope):
    b, l = cq.shape[:2]
    q = (rms_norm(cq, p['gq']) @ p['w_uq']).reshape(b, l, MLA_HEADS, MLA_QK)
    kv = (rms_norm(ckv, p['gkv']) @ p['w_ukv']).reshape(b, l, MLA_HEADS, MLA_NOPE + MLA_V)
    k_nope, v = kv[..., :MLA_NOPE], kv[..., MLA_NOPE:]
    k_rope = jnp.broadcast_to(kr[:, :, None, :], (b, l, MLA_HEADS, MLA_ROPE))
    k = jnp.concatenate([k_nope, k_rope], axis=-1)
    q = rms_norm(q, p['qn'])
    k = rms_norm(k, p['kn'])
    if rope is not None:
        cos, sin = rope
        q = jnp.concatenate([q[..., :MLA_NOPE], apply_rope(q[..., MLA_NOPE:], cos, sin)], axis=-1)
        k = jnp.concatenate([k[..., :MLA_NOPE], apply_rope(k[..., MLA_NOPE:], cos, sin)], axis=-1)
    return q, k, v


def mla_mixer(lat_parts, ctx_parts, p, rope, need_ctx):
    b, l = lat_parts[0].shape[:2]
    scale = MLA_QK ** -0.5
    q_l, k_l, v_l = mla_qkv(*lat_parts, p, rope)
    q_c, k_c, v_c = mla_qkv(*ctx_parts, p, None)
    k_all = jnp.concatenate([k_c, k_l], axis=1)
    v_all = jnp.concatenate([v_c, v_l], axis=1)
    o_lat = attend_latent(q_l[:, :, :, None], k_all, v_all, scale).reshape(b, l, MLA_HEADS * MLA_V)
    o_ctx = None
    if need_ctx:
        o_ctx = attend_dense(q_c[:, :, :, None], k_c, v_c, scale).reshape(b, q_c.shape[1], MLA_HEADS * MLA_V)
    return o_lat, o_ctx


def even_mixer(h_lat, h_ctx, p, rope, need_ctx):
    cuts = [S5_WIDTH, S5_WIDTH + MLA_Q_RANK, S5_WIDTH + MLA_Q_RANK + MLA_KV_RANK]
    u_l, cq_l, ckv_l, kr_l = jnp.split(h_lat @ p['w_in'], cuts, axis=-1)
    u_c, cq_c, ckv_c, kr_c = jnp.split(h_ctx @ p['w_in'], cuts, axis=-1)
    y_l, y_c = s5_mixer(u_l, u_c, p, need_ctx)
    o_l, o_c = mla_mixer((cq_l, ckv_l, kr_l), (cq_c, ckv_c, kr_c), p, rope, need_ctx)
    out_lat = jnp.concatenate([y_l, o_l], axis=-1) @ p['w_out']
    out_ctx = jnp.concatenate([y_c, o_c], axis=-1) @ p['w_out'] if need_ctx else None
    return out_lat, out_ctx


def gqa_qkv(h, p, rope):
    b, l = h.shape[:2]
    q, k, v = jnp.split(h @ p['w_qkv'], [ODD_Q, ODD_Q + ODD_KV], axis=-1)
    q = rms_norm(q.reshape(b, l, GQA_HEADS, GQA_DIM), p['qn'])
    k = rms_norm(k.reshape(b, l, GQA_KV_HEADS, GQA_DIM), p['kn'])
    v = v.reshape(b, l, GQA_KV_HEADS, GQA_DIM)
    if rope is not None:
        q = apply_rope(q, *rope)
        k = apply_rope(k, *rope)
    return q.reshape(b, l, GQA_KV_HEADS, GQA_GROUP, GQA_DIM), k, v


def odd_mixer(h_lat, h_ctx, p, rope, need_ctx):
    b, l = h_lat.shape[:2]
    scale = GQA_DIM ** -0.5
    q_l, k_l, v_l = gqa_qkv(h_lat, p, rope)
    q_c, k_c, v_c = gqa_qkv(h_ctx, p, None)
    k_all = jnp.concatenate([k_c, k_l], axis=1)
    v_all = jnp.concatenate([v_c, v_l], axis=1)
    out_lat = attend_latent(q_l, k_all, v_all, scale).reshape(b, l, ODD_MIX) @ p['w_out']
    out_ctx = None
    if need_ctx:
        out_ctx = attend_dense(q_c, k_c, v_c, scale).reshape(b, h_ctx.shape[1], ODD_MIX) @ p['w_out']
    return out_lat, out_ctx


def hier_moe(h, p):
    t, d = h.shape
    n_rows = t * MOE_TOP_K
    n_blocks = (n_rows + MOE_EXPERTS * (MOE_BLOCK - 1) + MOE_BLOCK - 1) // MOE_BLOCK
    lg1 = jnp.einsum('td,dg->tg', h, p['w_r1'], preferred_element_type=jnp.float32) + p['b_r1']
    p1 = jax.nn.softmax(lg1, axis=-1)
    grp = jnp.argmax(lg1, axis=-1)
    tok_idx = jnp.arange(t)
    p_grp = p1[tok_idx, grp]
    lg2 = (jnp.einsum('td,de->te', h, p['w_r2'], preferred_element_type=jnp.float32) + p['b_r2'])
    lg2 = lg2.reshape(t, MOE_GROUPS, MOE_PER_GROUP)[tok_idx, grp]
    top_v, top_i = lax.top_k(lg2, MOE_TOP_K)
    gates = p_grp[:, None] * jax.nn.softmax(top_v, axis=-1)
    expert = (grp[:, None] * MOE_PER_GROUP + top_i).reshape(-1)
    token = jnp.repeat(tok_idx, MOE_TOP_K)
    onehot = jax.nn.one_hot(expert, MOE_EXPERTS, dtype=jnp.int32)
    rank = jnp.cumsum(onehot, axis=0)[jnp.arange(n_rows), expert] - 1
    counts = onehot.sum(axis=0)
    padded = (counts + MOE_BLOCK - 1) // MOE_BLOCK * MOE_BLOCK
    pad_end = jnp.cumsum(padded)
    dest = (pad_end - padded)[expert] + rank
    xs = jnp.zeros((n_blocks * MOE_BLOCK, d), h.dtype).at[dest].set(h[token])
    blk_expert = jnp.minimum(jnp.searchsorted(pad_end, jnp.arange(n_blocks) * MOE_BLOCK, side='right'),
                             MOE_EXPERTS - 1)

    def run_block(args):
        xb, e = args
        return (jax.nn.silu(xb @ p['w_gate'][e]) * (xb @ p['w_up'][e])) @ p['w_down'][e]

    ys = lax.map(run_block, (xs.reshape(n_blocks, MOE_BLOCK, d), blk_expert)).reshape(-1, d)
    return (ys[dest] * gates.reshape(-1, 1).astype(h.dtype)).reshape(t, MOE_TOP_K, d).sum(axis=1)


def setup_inputs(seed: int = 0) -> dict:
    key = jax.random.key(seed)
    ks = iter(jax.random.split(key, 40))

    def nrm(shape, std):
        return std * jax.random.normal(next(ks), shape, jnp.float32)

    def gain(shape):
        return 1.0 + nrm(shape, 0.05)

    n_idx = jnp.arange(S5_STATE, dtype=jnp.float32)
    return {
        'x': nrm((BATCH, SEQ, D_MODEL), 1.0),
        'c': nrm((BATCH, D_MODEL), 1.0),
        'ctx': nrm((BATCH, CTX_LEN, D_MODEL), 1.0),
        'c_ctx': nrm((D_MODEL,), 1.0),
        'w_ada': nrm((DEPTH, D_MODEL, N_MOD * D_MODEL), 0.5 * D_MODEL ** -0.5),
        'b_ada': nrm((DEPTH, N_MOD * D_MODEL), 0.02),
        'norm1_g': gain((DEPTH, D_MODEL)),
        'norm2_g': gain((DEPTH, D_MODEL)),
        'w_in_e': nrm((N_EVEN, D_MODEL, EVEN_IN), D_MODEL ** -0.5),
        'w_out_e': nrm((N_EVEN, EVEN_MIX, D_MODEL), EVEN_MIX ** -0.5),
        's5_a_re': -0.5 + nrm((N_EVEN, 2, S5_GROUPS, S5_STATE), 0.01),
        's5_a_im': jnp.pi * n_idx + nrm((N_EVEN, 2, S5_GROUPS, S5_STATE), 0.01),
        's5_log_dt': jax.random.uniform(next(ks), (N_EVEN, 2, S5_GROUPS), jnp.float32,
                                        minval=math.log(S5_DT_MIN), maxval=math.log(S5_DT_MAX)),
        's5_b_re': nrm((N_EVEN, 2, S5_GROUPS, S5_STATE, S5_GROUP), (2 * S5_GROUP) ** -0.5),
        's5_b_im': nrm((N_EVEN, 2, S5_GROUPS, S5_STATE, S5_GROUP), (2 * S5_GROUP) ** -0.5),
        's5_c_re': nrm((N_EVEN, 2, S5_GROUPS, S5_GROUP, S5_STATE), S5_STATE ** -0.5),
        's5_c_im': nrm((N_EVEN, 2, S5_GROUPS, S5_GROUP, S5_STATE), S5_STATE ** -0.5),
        's5_d': nrm((N_EVEN, S5_WIDTH), 0.5),
        's5_w_glu': nrm((N_EVEN, S5_WIDTH, S5_WIDTH), S5_WIDTH ** -0.5),
        'mla_gq': gain((N_EVEN, MLA_Q_RANK)),
        'mla_w_uq': nrm((N_EVEN, MLA_Q_RANK, MLA_HEADS * MLA_QK), MLA_Q_RANK ** -0.5),
        'mla_gkv': gain((N_EVEN, MLA_KV_RANK)),
        'mla_w_ukv': nrm((N_EVEN, MLA_KV_RANK, MLA_HEADS * (MLA_NOPE + MLA_V)), MLA_KV_RANK ** -0.5),
        'mla_qn': gain((N_EVEN, MLA_QK)),
        'mla_kn': gain((N_EVEN, MLA_QK)),
        'w_qkv_o': nrm((N_ODD, D_MODEL, ODD_IN), D_MODEL ** -0.5),
        'w_out_o': nrm((N_ODD, ODD_MIX, D_MODEL), ODD_MIX ** -0.5),
        'gqa_qn': gain((N_ODD, GQA_DIM)),
        'gqa_kn': gain((N_ODD, GQA_DIM)),
        'moe_w_r1': nrm((DEPTH, D_MODEL, MOE_GROUPS), D_MODEL ** -0.5),
        'moe_b_r1': nrm((DEPTH, MOE_GROUPS), 0.01),
        'moe_w_r2': nrm((DEPTH, D_MODEL, MOE_EXPERTS), D_MODEL ** -0.5),
        'moe_b_r2': nrm((DEPTH, MOE_EXPERTS), 0.01),
        'moe_w_gate': nrm((DEPTH, MOE_EXPERTS, D_MODEL, MOE_FF), D_MODEL ** -0.5),
        'moe_w_up': nrm((DEPTH, MOE_EXPERTS, D_MODEL, MOE_FF), D_MODEL ** -0.5),
        'moe_w_down': nrm((DEPTH, MOE_EXPERTS, MOE_FF, D_MODEL), MOE_FF ** -0.5),
    }


def reference(x, c, ctx, c_ctx, w_ada, b_ada, norm1_g, norm2_g, w_in_e, w_out_e,
              s5_a_re, s5_a_im, s5_log_dt, s5_b_re, s5_b_im, s5_c_re, s5_c_im, s5_d, s5_w_glu,
              mla_gq, mla_w_uq, mla_gkv, mla_w_ukv, mla_qn, mla_kn,
              w_qkv_o, w_out_o, gqa_qn, gqa_kn,
              moe_w_r1, moe_b_r1, moe_w_r2, moe_b_r2, moe_w_gate, moe_w_up, moe_w_down):
    b, n_lat, _ = x.shape
    rows = n_lat // GRID_W
    row = jnp.repeat(jnp.arange(rows, dtype=jnp.float32), GRID_W)
    col = jnp.tile(jnp.arange(GRID_W, dtype=jnp.float32), rows)
    rope_mla = rope_tables(row, col, MLA_ROPE)
    rope_gqa = rope_tables(row, col, GQA_DIM)
    sc = jax.nn.silu(c)
    sc_ctx = jax.nn.silu(c_ctx)
    x_lat, x_ctx = x, ctx
    for l in range(DEPTH):
        last = l == DEPTH - 1
        m_lat = [m[:, None, :] for m in ada_chunks(sc, w_ada[l], b_ada[l])]
        m_ctx = ada_chunks(sc_ctx, w_ada[l], b_ada[l])
        h_lat = modulate(rms_norm(x_lat, norm1_g[l]), m_lat[0], m_lat[1])
        h_ctx = modulate(rms_norm(x_ctx, norm1_g[l]), m_ctx[0], m_ctx[1])
        if l % 2 == 0:
            i = l // 2
            p = dict(w_in=w_in_e[i], w_out=w_out_e[i], a_re=s5_a_re[i], a_im=s5_a_im[i],
                     log_dt=s5_log_dt[i], b_re=s5_b_re[i], b_im=s5_b_im[i], c_re=s5_c_re[i],
                     c_im=s5_c_im[i], d=s5_d[i], w_glu=s5_w_glu[i], gq=mla_gq[i], w_uq=mla_w_uq[i],
                     gkv=mla_gkv[i], w_ukv=mla_w_ukv[i], qn=mla_qn[i], kn=mla_kn[i])
            mix_lat, mix_ctx = even_mixer(h_lat, h_ctx, p, rope_mla, not last)
        else:
            i = l // 2
            p = dict(w_qkv=w_qkv_o[i], w_out=w_out_o[i], qn=gqa_qn[i], kn=gqa_kn[i])
            mix_lat, mix_ctx = odd_mixer(h_lat, h_ctx, p, rope_gqa, not last)
        x_lat = x_lat + m_lat[2] * mix_lat
        p_moe = dict(w_r1=moe_w_r1[l], b_r1=moe_b_r1[l], w_r2=moe_w_r2[l], b_r2=moe_b_r2[l],
                     w_gate=moe_w_gate[l], w_up=moe_w_up[l], w_down=moe_w_down[l])
        g_lat = modulate(rms_norm(x_lat, norm2_g[l]), m_lat[3], m_lat[4])
        if last:
            f_lat = hier_moe(g_lat.reshape(-1, D_MODEL), p_moe).reshape(x_lat.shape)
        else:
            x_ctx = x_ctx + m_ctx[2] * mix_ctx
            g_ctx = modulate(rms_norm(x_ctx, norm2_g[l]), m_ctx[3], m_ctx[4])
            f_all = hier_moe(jnp.concatenate([g_lat.reshape(-1, D_MODEL), g_ctx.reshape(-1, D_MODEL)], axis=0), p_moe)
            f_lat = f_all[:b * n_lat].reshape(x_lat.shape)
            x_ctx = x_ctx + m_ctx[5] * f_all[b * n_lat:].reshape(x_ctx.shape)
        x_lat = x_lat + m_lat[5] * f_lat
    return x_lat
```

```python
import functools
import math

import jax
import jax.numpy as jnp
import numpy as np
from jax import lax
from jax.experimental import pallas as pl
from jax.experimental.pallas import tpu as pltpu

F32 = jnp.float32
BF16 = jnp.bfloat16
HIGHEST = lax.Precision.HIGHEST

D_MODEL = 1024
N_MOD = 6
RMS_EPS = 1e-6
ROPE_BASE = 10000.0
GRID_W = 64

S5_WIDTH = 512
S5_GROUP = 16
S5_GROUPS = 32
S5_STATE = 64
S5_QUARTERS = 4
S5_GPQ = S5_GROUPS // S5_QUARTERS
S5_COLS = S5_GROUPS * S5_STATE

MLA_V = 64
MLA_HEADS = 8
MLA_NOPE = 64
MLA_ROPE = 32
MLA_QK = MLA_NOPE + MLA_ROPE
MLA_Q_RANK = 256
MLA_KV_RANK = 128

HEAD_LANES = 128
GQA_DIM = 128
GQA_HEADS = 8
GQA_KV_HEADS = 4

MOE_GROUPS = 4
MOE_PER_GROUP = 8
MOE_EXPERTS = 32
MOE_FF = 512

TM = 256
TQ = 256
TK = 512
S5_TC = 128
MOE_BM = 256
CMB_TM = 128
VMEM_LIMIT = 56 * 1024 * 1024
NEG_BIG = -1e30


def _cparams(n_axes):
    return pltpu.CompilerParams(dimension_semantics=("arbitrary",) * n_axes, vmem_limit_bytes=VMEM_LIMIT)


def _rms(x, width):
    return x * lax.rsqrt(jnp.sum(x * x, axis=-1, keepdims=True) * (1.0 / width) + RMS_EPS)


def _sigmoid(x):
    return 1.0 / (1.0 + jnp.exp(-x))


def _rope(x, cos, sin_signed, half):
    lane = lax.broadcasted_iota(jnp.int32, x.shape, 1)
    lo = (lane & (2 * half - 1)) < half
    partner = jnp.where(lo, pltpu.roll(x, HEAD_LANES - half, 1), pltpu.roll(x, half, 1))
    return x * cos + partner * sin_signed


def _ada_kernel(c_ref, w_ref, b_ref, o_ref):
    c = c_ref[...]
    sc = c * _sigmoid(c)
    o_ref[0] = jnp.dot(sc, w_ref[0], precision=HIGHEST, preferred_element_type=F32) + b_ref[0]


def _ada_table(cond, w_ada, b_ada):
    depth = w_ada.shape[0]
    n_out = w_ada.shape[2]
    tn = 1536
    return pl.pallas_call(
        _ada_kernel,
        grid=(depth, n_out // tn),
        in_specs=[
            pl.BlockSpec((8, D_MODEL), lambda l, j: (0, 0)),
            pl.BlockSpec((1, D_MODEL, tn), lambda l, j: (l, 0, j)),
            pl.BlockSpec((1, 1, tn), lambda l, j: (l, 0, j)),
        ],
        out_specs=pl.BlockSpec((1, 8, tn), lambda l, j: (l, 0, j)),
        out_shape=jax.ShapeDtypeStruct((depth, 8, n_out), F32),
        compiler_params=_cparams(2),
        name="ada_table",
    )(cond, w_ada, b_ada.reshape(depth, 1, n_out))


def _norm_modulate(x, gain, mod, k):
    shift = mod[:, k * D_MODEL:(k + 1) * D_MODEL]
    scale = mod[:, (k + 1) * D_MODEL:(k + 2) * D_MODEL]
    return _rms(x, D_MODEL) * gain * (1.0 + scale) + shift


class _Rows:
    def __init__(self, batch, n_lat, n_ctx, tile):
        assert n_lat % tile == 0 and n_ctx % tile == 0
        self.tile = tile
        self.lat_tiles = batch * n_lat // tile
        self.per_batch = n_lat // tile
        self.ctx_per_batch = n_ctx // tile
        self.all_tiles = self.lat_tiles + batch * n_ctx // tile
        self.batch = batch

    def mod_row(self, i):
        return jnp.where(i < self.lat_tiles, i // self.per_batch, self.batch)

    def rope_tile(self, i):
        return jnp.where(i < self.lat_tiles, i % self.per_batch, self.per_batch)


def _in_odd_kernel(x_ref, mod_ref, g1_ref, w_ref, qn_ref, kn_ref, cos_ref, sin_ref, q_ref, k_ref, v_ref):
    h = _norm_modulate(x_ref[...], g1_ref[...], mod_ref[0], 0)
    y = jnp.dot(h.astype(BF16), w_ref[...], preferred_element_type=F32)
    cos = cos_ref[...]
    sin = sin_ref[...]
    q_scale = GQA_DIM ** -0.5
    for hd in range(GQA_HEADS):
        qh = _rms(y[:, hd * GQA_DIM:(hd + 1) * GQA_DIM], GQA_DIM) * qn_ref[...]
        q_ref[:, hd * GQA_DIM:(hd + 1) * GQA_DIM] = (_rope(qh, cos, sin, GQA_DIM // 4) * q_scale).astype(BF16)
    k0 = GQA_HEADS * GQA_DIM
    for hd in range(GQA_KV_HEADS):
        kh = _rms(y[:, k0 + hd * GQA_DIM:k0 + (hd + 1) * GQA_DIM], GQA_DIM) * kn_ref[...]
        k_ref[:, hd * GQA_DIM:(hd + 1) * GQA_DIM] = _rope(kh, cos, sin, GQA_DIM // 4).astype(BF16)
    v0 = k0 + GQA_KV_HEADS * GQA_DIM
    v_ref[...] = y[:, v0:v0 + GQA_KV_HEADS * GQA_DIM].astype(BF16)


def _in_odd(x, mods, g1, w_qkv, qn, kn, cos_tab, sin_tab, rows):
    n_rows = x.shape[0]
    n_q = GQA_HEADS * GQA_DIM
    n_kv = GQA_KV_HEADS * GQA_DIM
    row = lambda i: (i, 0)
    fixed = lambda i: (0, 0)
    return pl.pallas_call(
        _in_odd_kernel,
        grid=(rows.all_tiles,),
        in_specs=[
            pl.BlockSpec((TM, D_MODEL), row),
            pl.BlockSpec((1, 1, N_MOD * D_MODEL), lambda i: (rows.mod_row(i), 0, 0)),
            pl.BlockSpec((1, D_MODEL), fixed),
            pl.BlockSpec((D_MODEL, n_q + 2 * n_kv), fixed),
            pl.BlockSpec((1, GQA_DIM), fixed),
            pl.BlockSpec((1, GQA_DIM), fixed),
            pl.BlockSpec((TM, HEAD_LANES), lambda i: (rows.rope_tile(i), 0)),
            pl.BlockSpec((TM, HEAD_LANES), lambda i: (rows.rope_tile(i), 0)),
        ],
        out_specs=[
            pl.BlockSpec((TM, n_q), row),
            pl.BlockSpec((TM, n_kv), row),
            pl.BlockSpec((TM, n_kv), row),
        ],
        out_shape=[
            jax.ShapeDtypeStruct((n_rows, n_q), BF16),
            jax.ShapeDtypeStruct((n_rows, n_kv), BF16),
            jax.ShapeDtypeStruct((n_rows, n_kv), BF16),
        ],
        compiler_params=_cparams(1),
        name="in_proj_gqa",
    )(x, mods, g1, w_qkv, qn, kn, cos_tab, sin_tab)


def _in_even_kernel(x_ref, mod_ref, g1_ref, w_ref, gq_ref, wuq_ref, gkv_ref, wk_ref, wv_ref, qn_ref, kn_ref,
                    cos_ref, sin_ref, u_ref, q_ref, k_ref, v_ref):
    h = _norm_modulate(x_ref[...], g1_ref[...], mod_ref[0], 0)
    y = jnp.dot(h.astype(BF16), w_ref[...], preferred_element_type=F32)
    u_ref[...] = y[:, :S5_WIDTH]
    c0 = S5_WIDTH
    cq = _rms(y[:, c0:c0 + MLA_Q_RANK], MLA_Q_RANK) * gq_ref[...]
    c1 = c0 + MLA_Q_RANK
    ckv = (_rms(y[:, c1:c1 + MLA_KV_RANK], MLA_KV_RANK) * gkv_ref[...]).astype(BF16)
    c2 = c1 + MLA_KV_RANK
    kr = pltpu.roll(y[:, c2:c2 + HEAD_LANES], MLA_NOPE, 1)
    q_all = jnp.dot(cq.astype(BF16), wuq_ref[...], preferred_element_type=F32)
    k_all = jnp.dot(ckv, wk_ref[...], preferred_element_type=F32)
    v_ref[...] = jnp.dot(ckv, wv_ref[...], preferred_element_type=F32).astype(BF16)
    cos = cos_ref[...]
    sin = sin_ref[...]
    lane = lax.broadcasted_iota(jnp.int32, kr.shape, 1)
    q_scale = MLA_QK ** -0.5
    for hd in range(MLA_HEADS):
        sl = slice(hd * HEAD_LANES, (hd + 1) * HEAD_LANES)
        qh = _rms(q_all[:, sl], MLA_QK) * qn_ref[...]
        q_ref[:, sl] = (_rope(qh, cos, sin, MLA_ROPE // 4) * q_scale).astype(BF16)
        kh = jnp.where(lane < MLA_NOPE, k_all[:, sl], kr)
        kh = _rms(kh, MLA_QK) * kn_ref[...]
        k_ref[:, sl] = _rope(kh, cos, sin, MLA_ROPE // 4).astype(BF16)


def _in_even(x, mods, g1, w_in, gq, w_uq, gkv, w_k, w_v, qn, kn, cos_tab, sin_tab, rows):
    n_rows = x.shape[0]
    n_h = MLA_HEADS * HEAD_LANES
    row = lambda i: (i, 0)
    fixed = lambda i: (0, 0)
    return pl.pallas_call(
        _in_even_kernel,
        grid=(rows.all_tiles,),
        in_specs=[
            pl.BlockSpec((TM, D_MODEL), row),
            pl.BlockSpec((1, 1, N_MOD * D_MODEL), lambda i: (rows.mod_row(i), 0, 0)),
            pl.BlockSpec((1, D_MODEL), fixed),
            pl.BlockSpec((D_MODEL, D_MODEL), fixed),
            pl.BlockSpec((1, MLA_Q_RANK), fixed),
            pl.BlockSpec((MLA_Q_RANK, n_h), fixed),
            pl.BlockSpec((1, MLA_KV_RANK), fixed),
            pl.BlockSpec((MLA_KV_RANK, n_h), fixed),
            pl.BlockSpec((MLA_KV_RANK, n_h), fixed),
            pl.BlockSpec((1, HEAD_LANES), fixed),
            pl.BlockSpec((1, HEAD_LANES), fixed),
            pl.BlockSpec((TM, HEAD_LANES), lambda i: (rows.rope_tile(i), 0)),
            pl.BlockSpec((TM, HEAD_LANES), lambda i: (rows.rope_tile(i), 0)),
        ],
        out_specs=[
            pl.BlockSpec((TM, S5_WIDTH), row),
            pl.BlockSpec((TM, n_h), row),
            pl.BlockSpec((TM, n_h), row),
            pl.BlockSpec((TM, n_h), row),
        ],
        out_shape=[
            jax.ShapeDtypeStruct((n_rows, S5_WIDTH), F32),
            jax.ShapeDtypeStruct((n_rows, n_h), BF16),
            jax.ShapeDtypeStruct((n_rows, n_h), BF16),
            jax.ShapeDtypeStruct((n_rows, n_h), BF16),
        ],
        compiler_params=_cparams(1),
        name="in_proj_s5_mla",
    )(x, mods, g1, w_in, gq, w_uq, gkv, w_k, w_v, qn, kn, cos_tab, sin_tab)


def _attn_kernel(*refs, seg_lens, group):
    q_ref = refs[0]
    o_ref = refs[-1]
    chunks = []
    for s, n in enumerate(seg_lens):
        tk = min(TK, n)
        assert n % tk == 0
        for c in range(n // tk):
            chunks.append((refs[1 + 2 * s], refs[2 + 2 * s], c * tk, tk))
    nt = (((1,), (1,)), ((), ()))
    for g in range(group):
        sl = slice(g * HEAD_LANES, (g + 1) * HEAD_LANES)
        q = q_ref[:, sl]
        m = l = acc = None
        for k_ref, v_ref, start, tk in chunks:
            s = lax.dot_general(q, k_ref[start:start + tk, :], nt, preferred_element_type=F32)
            s_max = jnp.max(s, axis=-1, keepdims=True)
            if m is None:
                m = s_max
                p = jnp.exp(s - m)
                l = jnp.sum(p, axis=-1, keepdims=True)
                acc = jnp.dot(p.astype(BF16), v_ref[start:start + tk, :], preferred_element_type=F32)
            else:
                m_new = jnp.maximum(m, s_max)
                alpha = jnp.exp(m - m_new)
                p = jnp.exp(s - m_new)
                l = alpha * l + jnp.sum(p, axis=-1, keepdims=True)
                acc = alpha * acc + jnp.dot(p.astype(BF16), v_ref[start:start + tk, :],
                                            preferred_element_type=F32)
                m = m_new
        o_ref[:, sl] = (acc / l).astype(BF16)


def _attention(q, k, v, batch, n_lat, n_ctx, kv_heads, group, latent_queries):
    ctx_blk0 = batch * n_lat // n_ctx
    width = group * HEAD_LANES
    ctx_kv = pl.BlockSpec((n_ctx, HEAD_LANES), lambda b, h, i: (ctx_blk0 + b, h))
    if latent_queries:
        tq, q_tiles = TQ, n_lat // TQ
        q_spec = pl.BlockSpec((tq, width), lambda b, h, i: (b * q_tiles + i, h))
        o_spec = q_spec
        seg_lens = (n_lat, n_ctx)
        lat_kv = pl.BlockSpec((n_lat, HEAD_LANES), lambda b, h, i: (b, h))
        kv_specs = [lat_kv, lat_kv, ctx_kv, ctx_kv]
        kv_args = (k, v, k, v)
        out_rows = batch * n_lat
    else:
        tq, q_tiles = n_ctx, 1
        q_spec = pl.BlockSpec((tq, width), lambda b, h, i: (ctx_blk0 + b, h))
        o_spec = pl.BlockSpec((tq, width), lambda b, h, i: (b, h))
        seg_lens = (n_ctx,)
        kv_specs = [ctx_kv, ctx_kv]
        kv_args = (k, v)
        out_rows = batch * n_ctx
    return pl.pallas_call(
        functools.partial(_attn_kernel, seg_lens=seg_lens, group=group),
        grid=(batch, kv_heads, q_tiles),
        in_specs=[q_spec] + kv_specs,
        out_specs=o_spec,
        out_shape=jax.ShapeDtypeStruct((out_rows, q.shape[1]), BF16),
        compiler_params=_cparams(3),
        name="attention_lat" if latent_queries else "attention_ctx",
    )(q, *kv_args)


def _attention_all(q, k, v, batch, n_lat, n_ctx, kv_heads, group, need_ctx):
    o = _attention(q, k, v, batch, n_lat, n_ctx, kv_heads, group, True)
    if need_ctx:
        o = jnp.concatenate([o, _attention(q, k, v, batch, n_lat, n_ctx, kv_heads, group, False)], axis=0)
    return o


def _s5_kernel(u_ref, wdr_ref, wdi_ref, wc_ref, ar_ref, ai_ref, y_ref, bur, bui, hr, hi):
    @pl.when(pl.program_id(0) == 0)
    def _():
        hr[...] = jnp.zeros_like(hr)
        hi[...] = jnp.zeros_like(hi)

    n_rows = u_ref.shape[0]
    fwd = (lax.broadcasted_iota(jnp.int32, (n_rows, S5_GPQ * S5_GROUP), 0) & 7) < 4
    qw = S5_GPQ * S5_GROUP
    qs = S5_GPQ * S5_STATE
    for q in range(S5_QUARTERS):
        uq = u_ref[:, q * qw:(q + 1) * qw]
        lhs = jnp.concatenate([jnp.where(fwd, uq, 0.0), jnp.where(fwd, 0.0, uq)], axis=1).astype(BF16)
        bur[:, q * qs:(q + 1) * qs] = jnp.dot(lhs, wdr_ref[q], preferred_element_type=F32)
        bui[:, q * qs:(q + 1) * qs] = jnp.dot(lhs, wdi_ref[q], preferred_element_type=F32)

    steps = n_rows // 8
    for cb in range(S5_COLS // qs):
        cols = slice(cb * qs, (cb + 1) * qs)
        a_r = ar_ref[:, cols]
        a_i = ai_ref[:, cols]

        def step(t, carry, cols=cols, a_r=a_r, a_i=a_i):
            h_r, h_i = carry
            r0 = pl.multiple_of(t * 8, 8)
            n_r = a_r * h_r - a_i * h_i + bur[pl.ds(r0, 8), cols]
            n_i = a_r * h_i + a_i * h_r + bui[pl.ds(r0, 8), cols]
            bur[pl.ds(r0, 8), cols] = n_r
            bui[pl.ds(r0, 8), cols] = n_i
            return n_r, n_i

        h_r, h_i = lax.fori_loop(0, steps, step, (hr[:, cols], hi[:, cols]), unroll=8)
        hr[:, cols] = h_r
        hi[:, cols] = h_i

    for q in range(S5_QUARTERS):
        hq = jnp.concatenate([bur[:, q * qs:(q + 1) * qs], bui[:, q * qs:(q + 1) * qs]], axis=1).astype(BF16)
        o = jnp.dot(hq, wc_ref[q], preferred_element_type=F32)
        y_ref[:, q * qw:(q + 1) * qw] = jnp.where(fwd, o[:, :qw], o[:, qw:])


def _s5_scan(useq, wdr, wdi, wc, a_r, a_i):
    n_rows = useq.shape[0]
    blk = S5_TC * 8
    assert n_rows % blk == 0
    qs = S5_GPQ * S5_STATE
    qw = S5_GPQ * S5_GROUP
    return pl.pallas_call(
        _s5_kernel,
        grid=(n_rows // blk,),
        in_specs=[
            pl.BlockSpec((blk, S5_WIDTH), lambda i: (i, 0)),
            pl.BlockSpec((S5_QUARTERS, 2 * qw, qs), lambda i: (0, 0, 0)),
            pl.BlockSpec((S5_QUARTERS, 2 * qw, qs), lambda i: (0, 0, 0)),
            pl.BlockSpec((S5_QUARTERS, 2 * qs, 2 * qw), lambda i: (0, 0, 0)),
            pl.BlockSpec((8, S5_COLS), lambda i: (0, 0)),
            pl.BlockSpec((8, S5_COLS), lambda i: (0, 0)),
        ],
        out_specs=pl.BlockSpec((blk, S5_WIDTH), lambda i: (i, 0)),
        out_shape=jax.ShapeDtypeStruct((n_rows, S5_WIDTH), F32),
        scratch_shapes=[
            pltpu.VMEM((blk, S5_COLS), F32),
            pltpu.VMEM((blk, S5_COLS), F32),
            pltpu.VMEM((8, S5_COLS), F32),
            pltpu.VMEM((8, S5_COLS), F32),
        ],
        compiler_params=_cparams(1),
        name="s5_scan",
    )(useq, wdr, wdi, wc, a_r, a_i)


def _s5_params(a_re, a_im, log_dt, b_re, b_im, c_re, c_im):
    dt = jnp.exp(log_dt)[..., None]
    mag = jnp.exp(dt * a_re)
    th = dt * a_im
    ab_r, ab_i = mag * jnp.cos(th), mag * jnp.sin(th)
    den = a_re * a_re + a_im * a_im
    nr = ab_r - 1
    z_r = (nr * a_re + ab_i * a_im) / den
    z_i = (ab_i * a_re - nr * a_im) / den
    bb_r = z_r[..., None] * b_re - z_i[..., None] * b_im
    bb_i = z_r[..., None] * b_im + z_i[..., None] * b_re
    eye = jnp.eye(S5_GPQ, dtype=F32)
    qs = S5_GPQ * S5_STATE
    qw = S5_GPQ * S5_GROUP

    def drive(bb):
        bq = bb.reshape(2, S5_QUARTERS, S5_GPQ, S5_STATE, S5_GROUP)
        return jnp.einsum('dqgnp,gh->qdgphn', bq, eye).reshape(S5_QUARTERS, 2 * qw, qs).astype(BF16)

    def readout(c):
        cq = c.reshape(2, S5_QUARTERS, S5_GPQ, S5_GROUP, S5_STATE)
        return jnp.einsum('dqgpn,gh->qgndhp', cq, eye).reshape(S5_QUARTERS, qs, 2 * qw)

    wc = jnp.concatenate([readout(c_re), -readout(c_im)], axis=1).astype(BF16)
    expand = lambda a: jnp.repeat(a.reshape(2, S5_COLS), 4, axis=0)
    return drive(bb_r), drive(bb_i), wc, expand(ab_r), expand(ab_i)


def _s5_mixer(u, params, batch, n_lat, n_ctx):
    assert batch == 4
    u_lat = u[:batch * n_lat].reshape(batch, n_lat, S5_WIDTH)
    u_ctx = u[batch * n_lat:].reshape(batch, n_ctx, S5_WIDTH)
    fwd = jnp.concatenate([u_ctx, u_lat], axis=1)
    rev = jnp.concatenate([u_ctx[:, ::-1], u_lat[:, ::-1]], axis=1)
    n_t = n_ctx + n_lat
    useq = jnp.concatenate([fwd, rev], axis=0).transpose(1, 0, 2).reshape(n_t * 8, S5_WIDTH)
    y = _s5_scan(useq, *params).reshape(n_t, 8, S5_WIDTH).transpose(1, 0, 2)
    y_f, y_r = y[:batch], y[batch:]
    to_rows = lambda lat, ctx: jnp.concatenate(
        [lat.reshape(batch * n_lat, S5_WIDTH), ctx.reshape(batch * n_ctx, S5_WIDTH)], axis=0)
    return to_rows(y_f[:, n_ctx:], y_f[:, :n_ctx]), to_rows(y_r[:, n_ctx:][:, ::-1], y_r[:, :n_ctx][:, ::-1])


def _route(logits):
    lane = lax.broadcasted_iota(jnp.int32, logits.shape, 1).astype(F32)
    first = lambda hit: jnp.min(jnp.where(hit, lane, float(HEAD_LANES)), axis=-1, keepdims=True)
    is_grp = lane < MOE_GROUPS
    l1 = jnp.where(is_grp, logits, NEG_BIG)
    m1 = jnp.max(l1, axis=-1, keepdims=True)
    grp = first(l1 == m1)
    p_grp = 1.0 / jnp.sum(jnp.where(is_grp, jnp.exp(l1 - m1), 0.0), axis=-1, keepdims=True)
    lo = MOE_GROUPS + grp * MOE_PER_GROUP
    l2 = jnp.where((lane >= lo) & (lane < lo + MOE_PER_GROUP), logits, NEG_BIG)
    v1 = jnp.max(l2, axis=-1, keepdims=True)
    i1 = first(l2 == v1)
    l2 = jnp.where(lane == i1, NEG_BIG, l2)
    v2 = jnp.max(l2, axis=-1, keepdims=True)
    i2 = first(l2 == v2)
    e2 = jnp.exp(v2 - v1)
    g1 = p_grp / (1.0 + e2)
    g2 = p_grp * e2 / (1.0 + e2)
    out = jnp.where(lane == 0, i1 - MOE_GROUPS, 0.0)
    out = jnp.where(lane == 1, i2 - MOE_GROUPS, out)
    out = jnp.where(lane == 2, g1, out)
    return jnp.where(lane == 3, g2, out)


def _out_kernel(*refs, even):
    if even:
        (x_ref, mod_ref, g2_ref, yf_ref, yr_ref, u_ref, d_ref, wglu_ref, o_ref, wa_ref, wb_ref, wr_ref, br_ref,
         xo_ref, g_ref, rt_ref) = refs
        u = u_ref[...]
        z = yf_ref[...] + yr_ref[...] + d_ref[...] * u
        gz = 0.5 * z * (1.0 + jnp.tanh(math.sqrt(2.0 / math.pi) * (z + 0.044715 * (z * z * z))))
        s5 = gz * _sigmoid(jnp.dot(gz.astype(BF16), wglu_ref[...], preferred_element_type=F32))
        mix = (jnp.dot(s5.astype(BF16), wa_ref[...], preferred_element_type=F32)
               + jnp.dot(o_ref[...], wb_ref[...], preferred_element_type=F32))
    else:
        x_ref, mod_ref, g2_ref, o_ref, wb_ref, wr_ref, br_ref, xo_ref, g_ref, rt_ref = refs
        mix = jnp.dot(o_ref[...], wb_ref[...], preferred_element_type=F32)
    mod = mod_ref[0]
    x = x_ref[...] + mod[:, 2 * D_MODEL:3 * D_MODEL] * mix
    xo_ref[...] = x
    g = _norm_modulate(x, g2_ref[...], mod, 3)
    g_ref[...] = g
    logits = jnp.dot(g, wr_ref[...], precision=HIGHEST, preferred_element_type=F32) + br_ref[...]
    rt_ref[...] = _route(logits)


def _out_proj(x, mods, g2, mixer_inputs, w_route, b_route, rows, n_tiles, even):
    row = lambda i: (i, 0)
    fixed = lambda i: (0, 0)
    n_rows = n_tiles * TM
    head_specs = [
        pl.BlockSpec((TM, D_MODEL), row),
        pl.BlockSpec((1, 1, N_MOD * D_MODEL), lambda i: (rows.mod_row(i), 0, 0)),
        pl.BlockSpec((1, D_MODEL), fixed),
    ]
    if even:
        mix_specs = [
            pl.BlockSpec((TM, S5_WIDTH), row), pl.BlockSpec((TM, S5_WIDTH), row), pl.BlockSpec((TM, S5_WIDTH), row),
            pl.BlockSpec((1, S5_WIDTH), fixed), pl.BlockSpec((S5_WIDTH, S5_WIDTH), fixed),
            pl.BlockSpec((TM, MLA_HEADS * HEAD_LANES), row),
            pl.BlockSpec((S5_WIDTH, D_MODEL), fixed), pl.BlockSpec((MLA_HEADS * HEAD_LANES, D_MODEL), fixed),
        ]
    else:
        mix_specs = [pl.BlockSpec((TM, D_MODEL), row), pl.BlockSpec((D_MODEL, D_MODEL), fixed)]
    tail_specs = [pl.BlockSpec((D_MODEL, HEAD_LANES), fixed), pl.BlockSpec((1, HEAD_LANES), fixed)]
    return pl.pallas_call(
        functools.partial(_out_kernel, even=even),
        grid=(n_tiles,),
        in_specs=head_specs + mix_specs + tail_specs,
        out_specs=[pl.BlockSpec((TM, D_MODEL), row), pl.BlockSpec((TM, D_MODEL), row),
                   pl.BlockSpec((TM, HEAD_LANES), row)],
        out_shape=[jax.ShapeDtypeStruct((n_rows, D_MODEL), F32), jax.ShapeDtypeStruct((n_rows, D_MODEL), F32),
                   jax.ShapeDtypeStruct((n_rows, HEAD_LANES), F32)],
        compiler_params=_cparams(1),
        name="out_proj_even" if even else "out_proj_odd",
    )(x, mods, g2, *mixer_inputs, w_route, b_route)


def _gather_rows(idx_ref, n, src_hbm, dst, sem):
    def issue(r, c):
        pltpu.make_async_copy(src_hbm.at[pl.ds(idx_ref[0, 0, r], 1), :], dst.at[pl.ds(r, 1), :], sem).start()
        return c
    lax.fori_loop(0, n, issue, 0, unroll=8)


def _wait_rows(n, src_hbm, dst, sem):
    pltpu.make_async_copy(src_hbm.at[pl.ds(0, n), :], dst, sem).wait()


def _moe_kernel(blk_e, n_used, tok_ref, tok_next_ref, g_hbm, wg_ref, wu_ref, wd_ref, y_ref,
                xs, wg_b, wu_b, wd_b, sem):
    i = pl.program_id(0)
    used = n_used[0]
    slot = i % 2

    @pl.when(i == 0)
    def _():
        _gather_rows(tok_ref, MOE_BM, g_hbm, xs.at[0], sem.at[0])

    @pl.when(i + 1 < used)
    def _():
        _gather_rows(tok_next_ref, MOE_BM, g_hbm, xs.at[1 - slot], sem.at[1 - slot])

    @pl.when((i == 0) | (blk_e[i] != blk_e[jnp.maximum(i - 1, 0)]))
    def _():
        wg_b[...] = wg_ref[0].astype(BF16)
        wu_b[...] = wu_ref[0].astype(BF16)
        wd_b[...] = wd_ref[0].astype(BF16)

    @pl.when(i < used)
    def _():
        _wait_rows(MOE_BM, g_hbm, xs.at[slot], sem.at[slot])
        x = xs[slot].astype(BF16)
        a = jnp.dot(x, wg_b[...], preferred_element_type=F32)
        b = jnp.dot(x, wu_b[...], preferred_element_type=F32)
        hmid = (a * _sigmoid(a) * b).astype(BF16)
        y_ref[...] = jnp.dot(hmid, wd_b[...], preferred_element_type=F32)

    @pl.when(i >= used)
    def _():
        y_ref[...] = jnp.zeros_like(y_ref)


def _moe_ffn(g, src_tok, blk_e, n_used, w_gate, w_up, w_down):
    n_blocks = blk_e.shape[0]
    tok3 = src_tok.reshape(n_blocks, 1, MOE_BM)
    last = n_blocks - 1
    grid_spec = pltpu.PrefetchScalarGridSpec(
        num_scalar_prefetch=2,
        grid=(n_blocks,),
        in_specs=[
            pl.BlockSpec((1, 1, MOE_BM), lambda i, be, nu: (i, 0, 0), memory_space=pltpu.SMEM),
            pl.BlockSpec((1, 1, MOE_BM), lambda i, be, nu: (jnp.minimum(i + 1, last), 0, 0),
                         memory_space=pltpu.SMEM),
            pl.BlockSpec(memory_space=pl.ANY),
            pl.BlockSpec((1, D_MODEL, MOE_FF), lambda i, be, nu: (be[i], 0, 0)),
            pl.BlockSpec((1, D_MODEL, MOE_FF), lambda i, be, nu: (be[i], 0, 0)),
            pl.BlockSpec((1, MOE_FF, D_MODEL), lambda i, be, nu: (be[i], 0, 0)),
        ],
        out_specs=pl.BlockSpec((MOE_BM, D_MODEL), lambda i, be, nu: (i, 0)),
        scratch_shapes=[
            pltpu.VMEM((2, MOE_BM, D_MODEL), F32),
            pltpu.VMEM((D_MODEL, MOE_FF), BF16),
            pltpu.VMEM((D_MODEL, MOE_FF), BF16),
            pltpu.VMEM((MOE_FF, D_MODEL), BF16),
            pltpu.SemaphoreType.DMA((2,)),
        ],
    )
    return pl.pallas_call(
        _moe_kernel,
        grid_spec=grid_spec,
        out_shape=jax.ShapeDtypeStruct((n_blocks * MOE_BM, D_MODEL), F32),
        compiler_params=_cparams(1),
        name="moe_ffn",
    )(blk_e, n_used, tok3, tok3, g, w_gate, w_up, w_down)


def _moe_plan(route, n_tok):
    expert = route[:n_tok, 0:2].astype(jnp.int32).reshape(-1)
    n_rows = 2 * n_tok
    n_blocks = (n_rows + MOE_EXPERTS * (MOE_BM - 1) + MOE_BM - 1) // MOE_BM
    onehot = (expert[:, None] == jnp.arange(MOE_EXPERTS, dtype=jnp.int32)[None, :]).astype(jnp.int32)
    csum = jnp.cumsum(onehot, axis=0)
    rank = jnp.take_along_axis(csum, expert[:, None], axis=1)[:, 0] - 1
    counts = csum[-1]
    padded = (counts + MOE_BM - 1) // MOE_BM * MOE_BM
    pad_end = jnp.cumsum(padded)
    dest = (pad_end - padded)[expert] + rank
    src_tok = jnp.zeros((n_blocks * MOE_BM,), jnp.int32).at[dest].set(jnp.arange(n_rows, dtype=jnp.int32) // 2)
    blk_e = jnp.minimum(jnp.searchsorted(pad_end, jnp.arange(n_blocks, dtype=jnp.int32) * MOE_BM, side='right'),
                        MOE_EXPERTS - 1).astype(jnp.int32)
    n_used = (pad_end[-1:] // MOE_BM).astype(jnp.int32)
    return src_tok, blk_e, n_used, dest


def _combine_kernel(dst_ref, dst_next_ref, x_ref, mod_ref, rt_ref, ys_hbm, o_ref, buf, sem):
    i = pl.program_id(0)
    n = pl.num_programs(0)
    slot = i % 2
    rows = 2 * CMB_TM

    @pl.when(i == 0)
    def _():
        _gather_rows(dst_ref, rows, ys_hbm, buf.at[0], sem.at[0])

    @pl.when(i + 1 < n)
    def _():
        _gather_rows(dst_next_ref, rows, ys_hbm, buf.at[1 - slot], sem.at[1 - slot])

    _wait_rows(rows, ys_hbm, buf.at[slot], sem.at[slot])
    rt = rt_ref[...]
    f = rt[:, 2:3] * buf[slot, :CMB_TM, :] + rt[:, 3:4] * buf[slot, CMB_TM:, :]
    o_ref[...] = x_ref[...] + mod_ref[0][:, 5 * D_MODEL:6 * D_MODEL] * f


def _combine(x, mods, route, ys, dest, rows, n_tok):
    n_tiles = n_tok // CMB_TM
    d3 = dest.reshape(n_tiles, CMB_TM, 2).transpose(0, 2, 1).reshape(n_tiles, 1, 2 * CMB_TM)
    last = n_tiles - 1
    sub = TM // CMB_TM
    return pl.pallas_call(
        _combine_kernel,
        grid=(n_tiles,),
        in_specs=[
            pl.BlockSpec((1, 1, 2 * CMB_TM), lambda i: (i, 0, 0), memory_space=pltpu.SMEM),
            pl.BlockSpec((1, 1, 2 * CMB_TM), lambda i: (jnp.minimum(i + 1, last), 0, 0), memory_space=pltpu.SMEM),
            pl.BlockSpec((CMB_TM, D_MODEL), lambda i: (i, 0)),
            pl.BlockSpec((1, 1, N_MOD * D_MODEL), lambda i: (rows.mod_row(i // sub), 0, 0)),
            pl.BlockSpec((CMB_TM, HEAD_LANES), lambda i: (i, 0)),
            pl.BlockSpec(memory_space=pl.ANY),
        ],
        out_specs=pl.BlockSpec((CMB_TM, D_MODEL), lambda i: (i, 0)),
        out_shape=jax.ShapeDtypeStruct((n_tok, D_MODEL), F32),
        scratch_shapes=[pltpu.VMEM((2, 2 * CMB_TM, D_MODEL), F32), pltpu.SemaphoreType.DMA((2,))],
        compiler_params=_cparams(1),
        name="moe_combine",
    )(d3, d3, x, mods, route, ys)


def _rope_tables(n_lat, d_rot, lane0):
    n_freq = d_rot // 4
    t = jnp.arange(n_lat, dtype=jnp.int32)
    row = (t // GRID_W).astype(F32)
    col = (t % GRID_W).astype(F32)
    inv = ROPE_BASE ** (-jnp.arange(n_freq, dtype=F32) / n_freq)
    ang = jnp.stack([row[:, None] * inv, col[:, None] * inv], axis=1)
    cos = jnp.broadcast_to(jnp.cos(ang)[:, :, None, :], (n_lat, 2, 2, n_freq)).reshape(n_lat, d_rot)
    sin = jnp.sin(ang)[:, :, None, :] * jnp.array([-1.0, 1.0], F32)[None, None, :, None]
    sin = sin.reshape(n_lat, d_rot)
    cos_full = jnp.ones((n_lat + TM, HEAD_LANES), F32).at[:n_lat, lane0:lane0 + d_rot].set(cos)
    sin_full = jnp.zeros((n_lat + TM, HEAD_LANES), F32).at[:n_lat, lane0:lane0 + d_rot].set(sin)
    return cos_full, sin_full


def _pad_heads(w, n_heads, width, lane0=0):
    lead = w.shape[:-1]
    w = w.reshape(*lead, n_heads, width)
    w = jnp.pad(w, [(0, 0)] * len(lead) + [(0, 0), (lane0, HEAD_LANES - lane0 - width)])
    return w.reshape(*lead, n_heads * HEAD_LANES)


def _pad_lanes(v, lane0=0):
    return jnp.pad(v, (lane0, HEAD_LANES - lane0 - v.shape[0])).reshape(1, HEAD_LANES)


def kernel(x, c, ctx, c_ctx, w_ada, b_ada, norm1_g, norm2_g, w_in_e, w_out_e, s5_a_re, s5_a_im, s5_log_dt, s5_b_re, s5_b_im, s5_c_re, s5_c_im, s5_d, s5_w_glu, mla_gq, mla_w_uq, mla_gkv, mla_w_ukv, mla_qn, mla_kn, w_qkv_o, w_out_o, gqa_qn, gqa_kn, moe_w_r1, moe_b_r1, moe_w_r2, moe_b_r2, moe_w_gate, moe_w_up, moe_w_down):
    batch, n_lat, _ = x.shape
    n_ctx = ctx.shape[1]
    depth = w_ada.shape[0]
    rows = _Rows(batch, n_lat, n_ctx, TM)
    n_lat_rows = batch * n_lat

    cond = jnp.zeros((8, D_MODEL), F32).at[:batch].set(c).at[batch].set(c_ctx)
    mods_all = _ada_table(cond, w_ada, b_ada).reshape(depth, 8, 1, N_MOD * D_MODEL)

    rope_gqa = _rope_tables(n_lat, GQA_DIM, 0)
    rope_mla = _rope_tables(n_lat, MLA_ROPE, MLA_NOPE)

    xs = jnp.concatenate([x.reshape(n_lat_rows, D_MODEL), ctx.reshape(batch * n_ctx, D_MODEL)], axis=0)

    for l in range(depth):
        last = l == depth - 1
        i = l // 2
        mods = mods_all[l]
        g1 = norm1_g[l].reshape(1, D_MODEL)
        g2 = norm2_g[l].reshape(1, D_MODEL)
        if l % 2 == 0:
            w_in = jnp.pad(w_in_e[i], ((0, 0), (0, D_MODEL - w_in_e.shape[2]))).astype(BF16)
            w_ukv = mla_w_ukv[i].reshape(MLA_KV_RANK, MLA_HEADS, MLA_NOPE + MLA_V)
            w_k = _pad_heads(w_ukv[..., :MLA_NOPE].reshape(MLA_KV_RANK, -1), MLA_HEADS, MLA_NOPE).astype(BF16)
            w_v = _pad_heads(w_ukv[..., MLA_NOPE:].reshape(MLA_KV_RANK, -1), MLA_HEADS, MLA_V).astype(BF16)
            u, q, k, v = _in_even(
                xs, mods, g1, w_in, mla_gq[i].reshape(1, -1),
                _pad_heads(mla_w_uq[i], MLA_HEADS, MLA_QK).astype(BF16), mla_gkv[i].reshape(1, -1), w_k, w_v,
                _pad_lanes(mla_qn[i]), _pad_lanes(mla_kn[i]), *rope_mla, rows)
            o = _attention_all(q, k, v, batch, n_lat, n_ctx, MLA_HEADS, 1, not last)
            s5p = _s5_params(s5_a_re[i], s5_a_im[i], s5_log_dt[i], s5_b_re[i], s5_b_im[i], s5_c_re[i], s5_c_im[i])
            y_f, y_r = _s5_mixer(u, s5p, batch, n_lat, n_ctx)
            w_b = w_out_e[i][S5_WIDTH:].reshape(MLA_HEADS, MLA_V, D_MODEL)
            w_b = jnp.pad(w_b, ((0, 0), (0, HEAD_LANES - MLA_V), (0, 0))).reshape(MLA_HEADS * HEAD_LANES, D_MODEL)
            mixer_inputs = (y_f, y_r, u, s5_d[i].reshape(1, -1), s5_w_glu[i].astype(BF16), o,
                            w_out_e[i][:S5_WIDTH].astype(BF16), w_b.astype(BF16))
        else:
            q, k, v = _in_odd(xs, mods, g1, w_qkv_o[i].astype(BF16), gqa_qn[i].reshape(1, -1),
                              gqa_kn[i].reshape(1, -1), *rope_gqa, rows)
            o = _attention_all(q, k, v, batch, n_lat, n_ctx, GQA_KV_HEADS, GQA_HEADS // GQA_KV_HEADS, not last)
            mixer_inputs = (o, w_out_o[i].astype(BF16))

        w_route = jnp.pad(jnp.concatenate([moe_w_r1[l], moe_w_r2[l]], axis=1),
                          ((0, 0), (0, HEAD_LANES - MOE_GROUPS - MOE_EXPERTS)))
        b_route = _pad_lanes(jnp.concatenate([moe_b_r1[l], moe_b_r2[l]]))
        n_tiles = rows.lat_tiles if last else rows.all_tiles
        n_tok = n_tiles * TM
        xs, g, route = _out_proj(xs, mods, g2, mixer_inputs, w_route, b_route, rows, n_tiles, l % 2 == 0)
        src_tok, blk_e, n_used, dest = _moe_plan(route, n_tok)
        ys = _moe_ffn(g, src_tok, blk_e, n_used, moe_w_gate[l], moe_w_up[l], moe_w_down[l])
        xs = _combine(xs, mods, route, ys, dest, rows, n_tok)
    return xs[:n_lat_rows].reshape(batch, n_lat, D_MODEL)
```

```python
import functools
import math

import jax
import jax.numpy as jnp
import numpy as np
from jax import lax
from jax.experimental import pallas as pl
from jax.experimental.pallas import tpu as pltpu

F32 = jnp.float32
BF16 = jnp.bfloat16
HIGHEST = lax.Precision.HIGHEST

D_MODEL = 1024
N_MOD = 6
RMS_EPS = 1e-6
ROPE_BASE = 10000.0
GRID_W = 64

S5_WIDTH = 512
S5_GROUP = 16
S5_GROUPS = 32
S5_STATE = 64
S5_QUARTERS = 4
S5_GPQ = S5_GROUPS // S5_QUARTERS
S5_COLS = S5_GROUPS * S5_STATE

MLA_V = 64
MLA_HEADS = 8
MLA_NOPE = 64
MLA_ROPE = 32
MLA_QK = MLA_NOPE + MLA_ROPE
MLA_Q_RANK = 256
MLA_KV_RANK = 128

HEAD_LANES = 128
GQA_DIM = 128
GQA_HEADS = 8
GQA_KV_HEADS = 4

MOE_GROUPS = 4
MOE_PER_GROUP = 8
MOE_EXPERTS = 32
MOE_FF = 512

TM = 256
TQ = 256
PV_CHUNK = 256
V_ONES = 16
LOG2_E = math.log2(math.e)
S5_TC = 128
MOE_BM = 256
CMB_TM = 128
VMEM_LIMIT = 56 * 1024 * 1024
NEG_BIG = -1e30


def _cparams(n_axes):
    return pltpu.CompilerParams(dimension_semantics=("arbitrary",) * n_axes, vmem_limit_bytes=VMEM_LIMIT)


def _rms(x, width):
    return x * lax.rsqrt(jnp.sum(x * x, axis=-1, keepdims=True) * (1.0 / width) + RMS_EPS)


def _sigmoid(x):
    return 1.0 / (1.0 + jnp.exp(-x))


def _rope(x, cos, sin_signed, half):
    lane = lax.broadcasted_iota(jnp.int32, x.shape, 1)
    lo = (lane & (2 * half - 1)) < half
    partner = jnp.where(lo, pltpu.roll(x, HEAD_LANES - half, 1), pltpu.roll(x, half, 1))
    return x * cos + partner * sin_signed


def _ada_kernel(c_ref, w_ref, b_ref, o_ref):
    c = c_ref[...]
    sc = c * _sigmoid(c)
    o_ref[0] = jnp.dot(sc, w_ref[0], precision=HIGHEST, preferred_element_type=F32) + b_ref[0]


def _ada_table(cond, w_ada, b_ada):
    depth = w_ada.shape[0]
    n_out = w_ada.shape[2]
    tn = 1536
    return pl.pallas_call(
        _ada_kernel,
        grid=(depth, n_out // tn),
        in_specs=[
            pl.BlockSpec((8, D_MODEL), lambda l, j: (0, 0)),
            pl.BlockSpec((1, D_MODEL, tn), lambda l, j: (l, 0, j)),
            pl.BlockSpec((1, 1, tn), lambda l, j: (l, 0, j)),
        ],
        out_specs=pl.BlockSpec((1, 8, tn), lambda l, j: (l, 0, j)),
        out_shape=jax.ShapeDtypeStruct((depth, 8, n_out), F32),
        compiler_params=_cparams(2),
        name="ada_table",
    )(cond, w_ada, b_ada.reshape(depth, 1, n_out))


def _norm_modulate(x, gain, mod, k):
    shift = mod[:, k * D_MODEL:(k + 1) * D_MODEL]
    scale = mod[:, (k + 1) * D_MODEL:(k + 2) * D_MODEL]
    return _rms(x, D_MODEL) * gain * (1.0 + scale) + shift


class _Rows:
    def __init__(self, batch, n_lat, n_ctx, tile):
        assert n_lat % tile == 0 and n_ctx % tile == 0
        self.tile = tile
        self.lat_tiles = batch * n_lat // tile
        self.per_batch = n_lat // tile
        self.ctx_per_batch = n_ctx // tile
        self.all_tiles = self.lat_tiles + batch * n_ctx // tile
        self.batch = batch

    def mod_row(self, i):
        return jnp.where(i < self.lat_tiles, i // self.per_batch, self.batch)

    def rope_tile(self, i):
        return jnp.where(i < self.lat_tiles, i % self.per_batch, self.per_batch)


def _store_vt(vt_ref, v, n_heads, dv):
    vt = v.T
    vr = dv + V_ONES
    ones = jnp.ones((V_ONES, vt.shape[1]), BF16)
    for hd in range(n_heads):
        vt_ref[hd * vr:hd * vr + dv, :] = vt[hd * dv:(hd + 1) * dv].astype(BF16)
        vt_ref[hd * vr + dv:(hd + 1) * vr, :] = ones


def _in_odd_kernel(x_ref, mod_ref, g1_ref, w_ref, qn_ref, kn_ref, cos_ref, sin_ref, q_ref, k_ref, vt_ref):
    h = _norm_modulate(x_ref[...], g1_ref[...], mod_ref[0], 0)
    y = jnp.dot(h.astype(BF16), w_ref[...], preferred_element_type=F32)
    cos = cos_ref[...]
    sin = sin_ref[...]
    q_scale = GQA_DIM ** -0.5 * LOG2_E
    for hd in range(GQA_HEADS):
        qh = _rms(y[:, hd * GQA_DIM:(hd + 1) * GQA_DIM], GQA_DIM) * qn_ref[...]
        q_ref[:, hd * GQA_DIM:(hd + 1) * GQA_DIM] = (_rope(qh, cos, sin, GQA_DIM // 4) * q_scale).astype(BF16)
    k0 = GQA_HEADS * GQA_DIM
    for hd in range(GQA_KV_HEADS):
        kh = _rms(y[:, k0 + hd * GQA_DIM:k0 + (hd + 1) * GQA_DIM], GQA_DIM) * kn_ref[...]
        k_ref[:, hd * GQA_DIM:(hd + 1) * GQA_DIM] = _rope(kh, cos, sin, GQA_DIM // 4).astype(BF16)
    v0 = k0 + GQA_KV_HEADS * GQA_DIM
    _store_vt(vt_ref, y[:, v0:v0 + GQA_KV_HEADS * GQA_DIM], GQA_KV_HEADS, GQA_DIM)


def _in_odd(x, mods, g1, w_qkv, qn, kn, cos_tab, sin_tab, rows):
    n_rows = x.shape[0]
    n_q = GQA_HEADS * GQA_DIM
    n_kv = GQA_KV_HEADS * GQA_DIM
    vt_rows = GQA_KV_HEADS * (GQA_DIM + V_ONES)
    row = lambda i: (i, 0)
    fixed = lambda i: (0, 0)
    return pl.pallas_call(
        _in_odd_kernel,
        grid=(rows.all_tiles,),
        in_specs=[
            pl.BlockSpec((TM, D_MODEL), row),
            pl.BlockSpec((1, 1, N_MOD * D_MODEL), lambda i: (rows.mod_row(i), 0, 0)),
            pl.BlockSpec((1, D_MODEL), fixed),
            pl.BlockSpec((D_MODEL, n_q + 2 * n_kv), fixed),
            pl.BlockSpec((1, GQA_DIM), fixed),
            pl.BlockSpec((1, GQA_DIM), fixed),
            pl.BlockSpec((TM, HEAD_LANES), lambda i: (rows.rope_tile(i), 0)),
            pl.BlockSpec((TM, HEAD_LANES), lambda i: (rows.rope_tile(i), 0)),
        ],
        out_specs=[
            pl.BlockSpec((TM, n_q), row),
            pl.BlockSpec((TM, n_kv), row),
            pl.BlockSpec((vt_rows, TM), lambda i: (0, i)),
        ],
        out_shape=[
            jax.ShapeDtypeStruct((n_rows, n_q), BF16),
            jax.ShapeDtypeStruct((n_rows, n_kv), BF16),
            jax.ShapeDtypeStruct((vt_rows, n_rows), BF16),
        ],
        compiler_params=_cparams(1),
        name="in_proj_gqa",
    )(x, mods, g1, w_qkv, qn, kn, cos_tab, sin_tab)


def _in_even_kernel(x_ref, mod_ref, g1_ref, w_ref, gq_ref, wuq_ref, gkv_ref, wk_ref, wv_ref, qn_ref, kn_ref,
                    cos_ref, sin_ref, u_ref, q_ref, k_ref, vt_ref):
    h = _norm_modulate(x_ref[...], g1_ref[...], mod_ref[0], 0)
    y = jnp.dot(h.astype(BF16), w_ref[...], preferred_element_type=F32)
    u_ref[...] = y[:, :S5_WIDTH]
    c0 = S5_WIDTH
    cq = _rms(y[:, c0:c0 + MLA_Q_RANK], MLA_Q_RANK) * gq_ref[...]
    c1 = c0 + MLA_Q_RANK
    ckv = (_rms(y[:, c1:c1 + MLA_KV_RANK], MLA_KV_RANK) * gkv_ref[...]).astype(BF16)
    c2 = c1 + MLA_KV_RANK
    kr = pltpu.roll(y[:, c2:c2 + HEAD_LANES], MLA_NOPE, 1)
    q_all = jnp.dot(cq.astype(BF16), wuq_ref[...], preferred_element_type=F32)
    k_all = jnp.dot(ckv, wk_ref[...], preferred_element_type=F32)
    _store_vt(vt_ref, jnp.dot(ckv, wv_ref[...], preferred_element_type=F32), MLA_HEADS, MLA_V)
    cos = cos_ref[...]
    sin = sin_ref[...]
    lane = lax.broadcasted_iota(jnp.int32, kr.shape, 1)
    q_scale = MLA_QK ** -0.5 * LOG2_E
    for hd in range(MLA_HEADS):
        sl = slice(hd * HEAD_LANES, (hd + 1) * HEAD_LANES)
        qh = _rms(q_all[:, sl], MLA_QK) * qn_ref[...]
        q_ref[:, sl] = (_rope(qh, cos, sin, MLA_ROPE // 4) * q_scale).astype(BF16)
        kh = jnp.where(lane < MLA_NOPE, k_all[:, sl], kr)
        kh = _rms(kh, MLA_QK) * kn_ref[...]
        k_ref[:, sl] = _rope(kh, cos, sin, MLA_ROPE // 4).astype(BF16)


def _in_even(x, mods, g1, w_in, gq, w_uq, gkv, w_k, w_v, qn, kn, cos_tab, sin_tab, rows):
    n_rows = x.shape[0]
    n_h = MLA_HEADS * HEAD_LANES
    vt_rows = MLA_HEADS * (MLA_V + V_ONES)
    row = lambda i: (i, 0)
    fixed = lambda i: (0, 0)
    return pl.pallas_call(
        _in_even_kernel,
        grid=(rows.all_tiles,),
        in_specs=[
            pl.BlockSpec((TM, D_MODEL), row),
            pl.BlockSpec((1, 1, N_MOD * D_MODEL), lambda i: (rows.mod_row(i), 0, 0)),
            pl.BlockSpec((1, D_MODEL), fixed),
            pl.BlockSpec((D_MODEL, D_MODEL), fixed),
            pl.BlockSpec((1, MLA_Q_RANK), fixed),
            pl.BlockSpec((MLA_Q_RANK, n_h), fixed),
            pl.BlockSpec((1, MLA_KV_RANK), fixed),
            pl.BlockSpec((MLA_KV_RANK, n_h), fixed),
            pl.BlockSpec((MLA_KV_RANK, MLA_HEADS * MLA_V), fixed),
            pl.BlockSpec((1, HEAD_LANES), fixed),
            pl.BlockSpec((1, HEAD_LANES), fixed),
            pl.BlockSpec((TM, HEAD_LANES), lambda i: (rows.rope_tile(i), 0)),
            pl.BlockSpec((TM, HEAD_LANES), lambda i: (rows.rope_tile(i), 0)),
        ],
        out_specs=[
            pl.BlockSpec((TM, S5_WIDTH), row),
            pl.BlockSpec((TM, n_h), row),
            pl.BlockSpec((TM, n_h), row),
            pl.BlockSpec((vt_rows, TM), lambda i: (0, i)),
        ],
        out_shape=[
            jax.ShapeDtypeStruct((n_rows, S5_WIDTH), F32),
            jax.ShapeDtypeStruct((n_rows, n_h), BF16),
            jax.ShapeDtypeStruct((n_rows, n_h), BF16),
            jax.ShapeDtypeStruct((vt_rows, n_rows), BF16),
        ],
        compiler_params=_cparams(1),
        name="in_proj_s5_mla",
    )(x, mods, g1, w_in, gq, w_uq, gkv, w_k, w_v, qn, kn, cos_tab, sin_tab)


def _attn_kernel(*refs, n_seg, units, dv):
    q_ref = refs[0]
    o_ref = refs[-1]
    segs = [(refs[1 + 2 * s], refs[2 + 2 * s]) for s in range(n_seg)]
    nt = (((1,), (1,)), ((), ()))
    vr = dv + V_ONES
    tq = q_ref.shape[0]
    outs = []
    for q_blks, k_blk, v_blk in units:
        q = jnp.concatenate([q_ref[:, b * HEAD_LANES:(b + 1) * HEAD_LANES] for b in q_blks], axis=0)
        scores = [lax.dot_general(k_ref[:, k_blk * HEAD_LANES:(k_blk + 1) * HEAD_LANES], q, nt,
                                  preferred_element_type=F32) for k_ref, _ in segs]
        m = functools.reduce(jnp.maximum, [jnp.max(s, axis=0, keepdims=True) for s in scores])
        acc = None
        for s, (_, vt_ref) in zip(scores, segs):
            for c in range(0, s.shape[0], PV_CHUNK):
                p = jnp.exp2(s[c:c + PV_CHUNK] - m).astype(BF16)
                part = jnp.dot(vt_ref[v_blk * vr:(v_blk + 1) * vr, c:c + PV_CHUNK], p, preferred_element_type=F32)
                acc = part if acc is None else acc + part
        o_t = acc[:dv] / acc[dv:dv + 1]
        outs += [o_t[:, j * tq:(j + 1) * tq] for j in range(len(q_blks))]
    o_ref[...] = jnp.concatenate(outs, axis=0).T.astype(BF16)


def _attention(q, k, vt, batch, n_lat, n_ctx, shared_kv, dv, latent_queries):
    ctx_blk0 = batch * n_lat // n_ctx
    n_steps = q.shape[1] // (2 * HEAD_LANES)
    k_width = HEAD_LANES if shared_kv else 2 * HEAD_LANES
    vt_rows = (dv + V_ONES) * (1 if shared_kv else 2)
    units = (((0, 1), 0, 0),) if shared_kv else (((0,), 0, 0), ((1,), 1, 1))
    tq_lat = TQ if shared_kv else 2 * TQ
    ctx_k = pl.BlockSpec((n_ctx, k_width), lambda b, h, i: (ctx_blk0 + b, h))
    ctx_v = pl.BlockSpec((vt_rows, n_ctx), lambda b, h, i: (h, ctx_blk0 + b))
    if latent_queries:
        tq, q_tiles = tq_lat, n_lat // tq_lat
        q_spec = pl.BlockSpec((tq, 2 * HEAD_LANES), lambda b, h, i: (b * q_tiles + i, h))
        o_spec = pl.BlockSpec((tq, 2 * dv), lambda b, h, i: (b * q_tiles + i, h))
        kv_specs = [pl.BlockSpec((n_lat, k_width), lambda b, h, i: (b, h)),
                    pl.BlockSpec((vt_rows, n_lat), lambda b, h, i: (h, b)), ctx_k, ctx_v]
        kv_args = (k, vt, k, vt)
        out_rows = batch * n_lat
    else:
        tq, q_tiles = n_ctx, 1
        q_spec = pl.BlockSpec((tq, 2 * HEAD_LANES), lambda b, h, i: (ctx_blk0 + b, h))
        o_spec = pl.BlockSpec((tq, 2 * dv), lambda b, h, i: (b, h))
        kv_specs = [ctx_k, ctx_v]
        kv_args = (k, vt)
        out_rows = batch * n_ctx
    return pl.pallas_call(
        functools.partial(_attn_kernel, n_seg=len(kv_args) // 2, units=units, dv=dv),
        grid=(batch, n_steps, q_tiles),
        in_specs=[q_spec] + kv_specs,
        out_specs=o_spec,
        out_shape=jax.ShapeDtypeStruct((out_rows, n_steps * 2 * dv), BF16),
        compiler_params=_cparams(3),
        name="attention_lat" if latent_queries else "attention_ctx",
    )(q, *kv_args)


def _attention_all(q, k, vt, batch, n_lat, n_ctx, shared_kv, dv, need_ctx):
    o = _attention(q, k, vt, batch, n_lat, n_ctx, shared_kv, dv, True)
    if need_ctx:
        o = jnp.concatenate([o, _attention(q, k, vt, batch, n_lat, n_ctx, shared_kv, dv, False)], axis=0)
    return o


def _s5_kernel(ulat_f, uctx_f, ulat_r, uctx_r, flip_ref, wdr_ref, wdi_ref, wc_ref, ar_ref, ai_ref,
               yf_lat, yf_ctx, yr_lat, yr_ctx, u_ref, y_ref, bur, bui, hr, hi, *, ctx_chunks):
    i = pl.program_id(0)
    batch, tc = ulat_f.shape[0], ulat_f.shape[1]

    @pl.when(i == 0)
    def _():
        hr[...] = jnp.zeros_like(hr)
        hi[...] = jnp.zeros_like(hi)

    qw = S5_GPQ * S5_GROUP
    qs = S5_GPQ * S5_STATE

    def load(src_f, src_r):
        for b in range(batch):
            u_f = src_f[b]
            u_r = jnp.dot(flip_ref[...], src_r[b].astype(BF16), preferred_element_type=F32)
            for q in range(S5_QUARTERS):
                u_ref[q, pl.ds(b, tc, stride=8), :] = u_f[:, q * qw:(q + 1) * qw]
                u_ref[q, pl.ds(batch + b, tc, stride=8), :] = u_r[:, q * qw:(q + 1) * qw]

    @pl.when(i < ctx_chunks)
    def _():
        load(uctx_f, uctx_r)

    @pl.when(i >= ctx_chunks)
    def _():
        load(ulat_f, ulat_r)

    n_rows = u_ref.shape[1]
    fwd = (lax.broadcasted_iota(jnp.int32, (n_rows, qw), 0) & 7) < 4
    for q in range(S5_QUARTERS):
        uq = u_ref[q]
        lhs = jnp.concatenate([jnp.where(fwd, uq, 0.0), jnp.where(fwd, 0.0, uq)], axis=1).astype(BF16)
        bur[:, q * qs:(q + 1) * qs] = jnp.dot(lhs, wdr_ref[q], preferred_element_type=F32)
        bui[:, q * qs:(q + 1) * qs] = jnp.dot(lhs, wdi_ref[q], preferred_element_type=F32)

    steps = n_rows // 8
    for cb in range(S5_COLS // qs):
        cols = slice(cb * qs, (cb + 1) * qs)
        a_r = ar_ref[:, cols]
        a_i = ai_ref[:, cols]

        def step(t, carry, cols=cols, a_r=a_r, a_i=a_i):
            h_r, h_i = carry
            r0 = pl.multiple_of(t * 8, 8)
            n_r = a_r * h_r - a_i * h_i + bur[pl.ds(r0, 8), cols]
            n_i = a_r * h_i + a_i * h_r + bui[pl.ds(r0, 8), cols]
            bur[pl.ds(r0, 8), cols] = n_r
            bui[pl.ds(r0, 8), cols] = n_i
            return n_r, n_i

        h_r, h_i = lax.fori_loop(0, steps, step, (hr[:, cols], hi[:, cols]), unroll=8)
        hr[:, cols] = h_r
        hi[:, cols] = h_i

    for q in range(S5_QUARTERS):
        hq = jnp.concatenate([bur[:, q * qs:(q + 1) * qs], bui[:, q * qs:(q + 1) * qs]], axis=1).astype(BF16)
        o = jnp.dot(hq, wc_ref[q], preferred_element_type=F32)
        y_ref[q] = jnp.where(fwd, o[:, :qw], o[:, qw:])

    def store(dst_f, dst_r):
        gather = lambda s: jnp.concatenate([y_ref[q, pl.ds(s, tc, stride=8), :] for q in range(S5_QUARTERS)], axis=1)
        for b in range(batch):
            dst_f[b] = gather(b)
            y_rev = gather(batch + b)
            top = y_rev.astype(BF16)
            rest = (y_rev - top.astype(F32)).astype(BF16)
            dst_r[b] = (jnp.dot(flip_ref[...], top, preferred_element_type=F32)
                        + jnp.dot(flip_ref[...], rest, preferred_element_type=F32))

    @pl.when(i < ctx_chunks)
    def _():
        store(yf_ctx, yr_ctx)

    @pl.when(i >= ctx_chunks)
    def _():
        store(yf_lat, yr_lat)


def _s5_scan(u_lat, u_ctx, wdr, wdi, wc, a_r, a_i):
    batch, n_lat, _ = u_lat.shape
    n_ctx = u_ctx.shape[1]
    tc = S5_TC
    assert 2 * batch == 8 and n_lat % tc == 0 and n_ctx % tc == 0
    nc, nl = n_ctx // tc, n_lat // tc
    blk = tc * 8
    qs = S5_GPQ * S5_STATE
    qw = S5_GPQ * S5_GROUP
    flip = jnp.eye(tc, dtype=BF16)[::-1]
    lat_f = lambda i: (0, jnp.clip(i - nc, 0, nl - 1), 0)
    ctx_f = lambda i: (0, jnp.clip(i, 0, nc - 1), 0)
    lat_r = lambda i: (0, jnp.clip(nl - 1 - (i - nc), 0, nl - 1), 0)
    ctx_r = lambda i: (0, jnp.clip(nc - 1 - i, 0, nc - 1), 0)
    tile = (batch, tc, S5_WIDTH)
    whole = lambda shape: pl.BlockSpec(shape, lambda i: (0,) * len(shape))
    return pl.pallas_call(
        functools.partial(_s5_kernel, ctx_chunks=nc),
        grid=(nc + nl,),
        in_specs=[
            pl.BlockSpec(tile, lat_f), pl.BlockSpec(tile, ctx_f), pl.BlockSpec(tile, lat_r), pl.BlockSpec(tile, ctx_r),
            whole((tc, tc)),
            whole((S5_QUARTERS, 2 * qw, qs)), whole((S5_QUARTERS, 2 * qw, qs)),
            whole((S5_QUARTERS, 2 * qs, 2 * qw)),
            whole((8, S5_COLS)), whole((8, S5_COLS)),
        ],
        out_specs=[pl.BlockSpec(tile, lat_f), pl.BlockSpec(tile, ctx_f), pl.BlockSpec(tile, lat_r),
                   pl.BlockSpec(tile, ctx_r)],
        out_shape=[jax.ShapeDtypeStruct(u_lat.shape, F32), jax.ShapeDtypeStruct(u_ctx.shape, F32),
                   jax.ShapeDtypeStruct(u_lat.shape, F32), jax.ShapeDtypeStruct(u_ctx.shape, F32)],
        scratch_shapes=[
            pltpu.VMEM((S5_QUARTERS, blk, qw), F32),
            pltpu.VMEM((S5_QUARTERS, blk, qw), F32),
            pltpu.VMEM((blk, S5_COLS), F32),
            pltpu.VMEM((blk, S5_COLS), F32),
            pltpu.VMEM((8, S5_COLS), F32),
            pltpu.VMEM((8, S5_COLS), F32),
        ],
        compiler_params=_cparams(1),
        name="s5_scan",
    )(u_lat, u_ctx, u_lat, u_ctx, flip, wdr, wdi, wc, a_r, a_i)


def _s5_params(a_re, a_im, log_dt, b_re, b_im, c_re, c_im):
    dt = jnp.exp(log_dt)[..., None]
    mag = jnp.exp(dt * a_re)
    th = dt * a_im
    ab_r, ab_i = mag * jnp.cos(th), mag * jnp.sin(th)
    den = a_re * a_re + a_im * a_im
    nr = ab_r - 1
    z_r = (nr * a_re + ab_i * a_im) / den
    z_i = (ab_i * a_re - nr * a_im) / den
    bb_r = z_r[..., None] * b_re - z_i[..., None] * b_im
    bb_i = z_r[..., None] * b_im + z_i[..., None] * b_re
    eye = jnp.eye(S5_GPQ, dtype=F32)
    qs = S5_GPQ * S5_STATE
    qw = S5_GPQ * S5_GROUP

    def drive(bb):
        bq = bb.reshape(2, S5_QUARTERS, S5_GPQ, S5_STATE, S5_GROUP)
        return jnp.einsum('dqgnp,gh->qdgphn', bq, eye).reshape(S5_QUARTERS, 2 * qw, qs).astype(BF16)

    def readout(c):
        cq = c.reshape(2, S5_QUARTERS, S5_GPQ, S5_GROUP, S5_STATE)
        return jnp.einsum('dqgpn,gh->qgndhp', cq, eye).reshape(S5_QUARTERS, qs, 2 * qw)

    wc = jnp.concatenate([readout(c_re), -readout(c_im)], axis=1).astype(BF16)
    expand = lambda a: jnp.repeat(a.reshape(2, S5_COLS), 4, axis=0)
    return drive(bb_r), drive(bb_i), wc, expand(ab_r), expand(ab_i)


def _s5_mixer(u, params, batch, n_lat, n_ctx):
    u_lat = u[:batch * n_lat].reshape(batch, n_lat, S5_WIDTH)
    u_ctx = u[batch * n_lat:].reshape(batch, n_ctx, S5_WIDTH)
    yf_lat, yf_ctx, yr_lat, yr_ctx = _s5_scan(u_lat, u_ctx, *params)
    to_rows = lambda lat, ctx: jnp.concatenate(
        [lat.reshape(batch * n_lat, S5_WIDTH), ctx.reshape(batch * n_ctx, S5_WIDTH)], axis=0)
    return to_rows(yf_lat, yf_ctx), to_rows(yr_lat, yr_ctx)


def _route(logits):
    lane = lax.broadcasted_iota(jnp.int32, logits.shape, 1).astype(F32)
    first = lambda hit: jnp.min(jnp.where(hit, lane, float(HEAD_LANES)), axis=-1, keepdims=True)
    is_grp = lane < MOE_GROUPS
    l1 = jnp.where(is_grp, logits, NEG_BIG)
    m1 = jnp.max(l1, axis=-1, keepdims=True)
    grp = first(l1 == m1)
    p_grp = 1.0 / jnp.sum(jnp.where(is_grp, jnp.exp(l1 - m1), 0.0), axis=-1, keepdims=True)
    lo = MOE_GROUPS + grp * MOE_PER_GROUP
    l2 = jnp.where((lane >= lo) & (lane < lo + MOE_PER_GROUP), logits, NEG_BIG)
    v1 = jnp.max(l2, axis=-1, keepdims=True)
    i1 = first(l2 == v1)
    l2 = jnp.where(lane == i1, NEG_BIG, l2)
    v2 = jnp.max(l2, axis=-1, keepdims=True)
    i2 = first(l2 == v2)
    e2 = jnp.exp(v2 - v1)
    g1 = p_grp / (1.0 + e2)
    g2 = p_grp * e2 / (1.0 + e2)
    out = jnp.where(lane == 0, i1 - MOE_GROUPS, 0.0)
    out = jnp.where(lane == 1, i2 - MOE_GROUPS, out)
    out = jnp.where(lane == 2, g1, out)
    return jnp.where(lane == 3, g2, out)


def _out_kernel(*refs, even):
    if even:
        (x_ref, mod_ref, g2_ref, yf_ref, yr_ref, u_ref, d_ref, wglu_ref, o_ref, wa_ref, wb_ref, wr_ref, br_ref,
         xo_ref, g_ref, rt_ref) = refs
        u = u_ref[...]
        z = yf_ref[...] + yr_ref[...] + d_ref[...] * u
        gz = 0.5 * z * (1.0 + jnp.tanh(math.sqrt(2.0 / math.pi) * (z + 0.044715 * (z * z * z))))
        s5 = gz * _sigmoid(jnp.dot(gz.astype(BF16), wglu_ref[...], preferred_element_type=F32))
        mix = (jnp.dot(s5.astype(BF16), wa_ref[...], preferred_element_type=F32)
               + jnp.dot(o_ref[...], wb_ref[...], preferred_element_type=F32))
    else:
        x_ref, mod_ref, g2_ref, o_ref, wb_ref, wr_ref, br_ref, xo_ref, g_ref, rt_ref = refs
        mix = jnp.dot(o_ref[...], wb_ref[...], preferred_element_type=F32)
    mod = mod_ref[0]
    x = x_ref[...] + mod[:, 2 * D_MODEL:3 * D_MODEL] * mix
    xo_ref[...] = x
    g = _norm_modulate(x, g2_ref[...], mod, 3)
    g_ref[...] = g
    logits = jnp.dot(g, wr_ref[...], precision=HIGHEST, preferred_element_type=F32) + br_ref[...]
    rt_ref[...] = _route(logits)


def _out_proj(x, mods, g2, mixer_inputs, w_route, b_route, rows, n_tiles, even):
    row = lambda i: (i, 0)
    fixed = lambda i: (0, 0)
    n_rows = n_tiles * TM
    head_specs = [
        pl.BlockSpec((TM, D_MODEL), row),
        pl.BlockSpec((1, 1, N_MOD * D_MODEL), lambda i: (rows.mod_row(i), 0, 0)),
        pl.BlockSpec((1, D_MODEL), fixed),
    ]
    if even:
        mix_specs = [
            pl.BlockSpec((TM, S5_WIDTH), row), pl.BlockSpec((TM, S5_WIDTH), row), pl.BlockSpec((TM, S5_WIDTH), row),
            pl.BlockSpec((1, S5_WIDTH), fixed), pl.BlockSpec((S5_WIDTH, S5_WIDTH), fixed),
            pl.BlockSpec((TM, MLA_HEADS * MLA_V), row),
            pl.BlockSpec((S5_WIDTH, D_MODEL), fixed), pl.BlockSpec((MLA_HEADS * MLA_V, D_MODEL), fixed),
        ]
    else:
        mix_specs = [pl.BlockSpec((TM, D_MODEL), row), pl.BlockSpec((D_MODEL, D_MODEL), fixed)]
    tail_specs = [pl.BlockSpec((D_MODEL, HEAD_LANES), fixed), pl.BlockSpec((1, HEAD_LANES), fixed)]
    return pl.pallas_call(
        functools.partial(_out_kernel, even=even),
        grid=(n_tiles,),
        in_specs=head_specs + mix_specs + tail_specs,
        out_specs=[pl.BlockSpec((TM, D_MODEL), row), pl.BlockSpec((TM, D_MODEL), row),
                   pl.BlockSpec((TM, HEAD_LANES), row)],
        out_shape=[jax.ShapeDtypeStruct((n_rows, D_MODEL), F32), jax.ShapeDtypeStruct((n_rows, D_MODEL), F32),
                   jax.ShapeDtypeStruct((n_rows, HEAD_LANES), F32)],
        compiler_params=_cparams(1),
        name="out_proj_even" if even else "out_proj_odd",
    )(x, mods, g2, *mixer_inputs, w_route, b_route)


def _gather_rows(idx_ref, n, src_hbm, dst, sem):
    def issue(r, c):
        pltpu.make_async_copy(src_hbm.at[pl.ds(idx_ref[0, 0, r], 1), :], dst.at[pl.ds(r, 1), :], sem).start()
        return c
    lax.fori_loop(0, n, issue, 0, unroll=8)


def _wait_rows(n, src_hbm, dst, sem):
    pltpu.make_async_copy(src_hbm.at[pl.ds(0, n), :], dst, sem).wait()


def _moe_kernel(blk_e, n_used, tok_ref, tok_next_ref, g_hbm, wg_ref, wu_ref, wd_ref, y_ref,
                xs, wg_b, wu_b, wd_b, sem):
    i = pl.program_id(0)
    used = n_used[0]
    slot = i % 2

    @pl.when(i == 0)
    def _():
        _gather_rows(tok_ref, MOE_BM, g_hbm, xs.at[0], sem.at[0])

    @pl.when(i + 1 < used)
    def _():
        _gather_rows(tok_next_ref, MOE_BM, g_hbm, xs.at[1 - slot], sem.at[1 - slot])

    @pl.when((i == 0) | (blk_e[i] != blk_e[jnp.maximum(i - 1, 0)]))
    def _():
        wg_b[...] = wg_ref[0].astype(BF16)
        wu_b[...] = wu_ref[0].astype(BF16)
        wd_b[...] = wd_ref[0].astype(BF16)

    @pl.when(i < used)
    def _():
        _wait_rows(MOE_BM, g_hbm, xs.at[slot], sem.at[slot])
        x = xs[slot].astype(BF16)
        a = jnp.dot(x, wg_b[...], preferred_element_type=F32)
        b = jnp.dot(x, wu_b[...], preferred_element_type=F32)
        hmid = (a * _sigmoid(a) * b).astype(BF16)
        y_ref[...] = jnp.dot(hmid, wd_b[...], preferred_element_type=F32)

    @pl.when(i >= used)
    def _():
        y_ref[...] = jnp.zeros_like(y_ref)


def _moe_ffn(g, src_tok, blk_e, n_used, w_gate, w_up, w_down):
    n_blocks = blk_e.shape[0]
    tok3 = src_tok.reshape(n_blocks, 1, MOE_BM)
    last = n_blocks - 1
    grid_spec = pltpu.PrefetchScalarGridSpec(
        num_scalar_prefetch=2,
        grid=(n_blocks,),
        in_specs=[
            pl.BlockSpec((1, 1, MOE_BM), lambda i, be, nu: (i, 0, 0), memory_space=pltpu.SMEM),
            pl.BlockSpec((1, 1, MOE_BM), lambda i, be, nu: (jnp.minimum(i + 1, last), 0, 0),
                         memory_space=pltpu.SMEM),
            pl.BlockSpec(memory_space=pl.ANY),
            pl.BlockSpec((1, D_MODEL, MOE_FF), lambda i, be, nu: (be[i], 0, 0)),
            pl.BlockSpec((1, D_MODEL, MOE_FF), lambda i, be, nu: (be[i], 0, 0)),
            pl.BlockSpec((1, MOE_FF, D_MODEL), lambda i, be, nu: (be[i], 0, 0)),
        ],
        out_specs=pl.BlockSpec((MOE_BM, D_MODEL), lambda i, be, nu: (i, 0)),
        scratch_shapes=[
            pltpu.VMEM((2, MOE_BM, D_MODEL), F32),
            pltpu.VMEM((D_MODEL, MOE_FF), BF16),
            pltpu.VMEM((D_MODEL, MOE_FF), BF16),
            pltpu.VMEM((MOE_FF, D_MODEL), BF16),
            pltpu.SemaphoreType.DMA((2,)),
        ],
    )
    return pl.pallas_call(
        _moe_kernel,
        grid_spec=grid_spec,
        out_shape=jax.ShapeDtypeStruct((n_blocks * MOE_BM, D_MODEL), F32),
        compiler_params=_cparams(1),
        name="moe_ffn",
    )(blk_e, n_used, tok3, tok3, g, w_gate, w_up, w_down)


def _moe_plan(route, n_tok):
    expert = route[:n_tok, 0:2].astype(jnp.int32).reshape(-1)
    n_rows = 2 * n_tok
    n_blocks = (n_rows + MOE_EXPERTS * (MOE_BM - 1) + MOE_BM - 1) // MOE_BM
    onehot = (expert[:, None] == jnp.arange(MOE_EXPERTS, dtype=jnp.int32)[None, :]).astype(jnp.int32)
    csum = jnp.cumsum(onehot, axis=0)
    rank = jnp.take_along_axis(csum, expert[:, None], axis=1)[:, 0] - 1
    counts = csum[-1]
    padded = (counts + MOE_BM - 1) // MOE_BM * MOE_BM
    pad_end = jnp.cumsum(padded)
    dest = (pad_end - padded)[expert] + rank
    src_tok = jnp.zeros((n_blocks * MOE_BM,), jnp.int32).at[dest].set(jnp.arange(n_rows, dtype=jnp.int32) // 2)
    blk_start = jnp.arange(n_blocks, dtype=jnp.int32) * MOE_BM
    blk_e = jnp.minimum(jnp.sum((pad_end[None, :] <= blk_start[:, None]).astype(jnp.int32), axis=1),
                        MOE_EXPERTS - 1)
    n_used = (pad_end[-1:] // MOE_BM).astype(jnp.int32)
    return src_tok, blk_e, n_used, dest


def _combine_kernel(dst_ref, dst_next_ref, x_ref, mod_ref, rt_ref, ys_hbm, o_ref, buf, sem):
    i = pl.program_id(0)
    n = pl.num_programs(0)
    slot = i % 2
    rows = 2 * CMB_TM

    @pl.when(i == 0)
    def _():
        _gather_rows(dst_ref, rows, ys_hbm, buf.at[0], sem.at[0])

    @pl.when(i + 1 < n)
    def _():
        _gather_rows(dst_next_ref, rows, ys_hbm, buf.at[1 - slot], sem.at[1 - slot])

    _wait_rows(rows, ys_hbm, buf.at[slot], sem.at[slot])
    rt = rt_ref[...]
    f = rt[:, 2:3] * buf[slot, :CMB_TM, :] + rt[:, 3:4] * buf[slot, CMB_TM:, :]
    o_ref[...] = x_ref[...] + mod_ref[0][:, 5 * D_MODEL:6 * D_MODEL] * f


def _combine(x, mods, route, ys, dest, rows, n_tok):
    n_tiles = n_tok // CMB_TM
    d3 = dest.reshape(n_tiles, CMB_TM, 2).transpose(0, 2, 1).reshape(n_tiles, 1, 2 * CMB_TM)
    last = n_tiles - 1
    sub = TM // CMB_TM
    return pl.pallas_call(
        _combine_kernel,
        grid=(n_tiles,),
        in_specs=[
            pl.BlockSpec((1, 1, 2 * CMB_TM), lambda i: (i, 0, 0), memory_space=pltpu.SMEM),
            pl.BlockSpec((1, 1, 2 * CMB_TM), lambda i: (jnp.minimum(i + 1, last), 0, 0), memory_space=pltpu.SMEM),
            pl.BlockSpec((CMB_TM, D_MODEL), lambda i: (i, 0)),
            pl.BlockSpec((1, 1, N_MOD * D_MODEL), lambda i: (rows.mod_row(i // sub), 0, 0)),
            pl.BlockSpec((CMB_TM, HEAD_LANES), lambda i: (i, 0)),
            pl.BlockSpec(memory_space=pl.ANY),
        ],
        out_specs=pl.BlockSpec((CMB_TM, D_MODEL), lambda i: (i, 0)),
        out_shape=jax.ShapeDtypeStruct((n_tok, D_MODEL), F32),
        scratch_shapes=[pltpu.VMEM((2, 2 * CMB_TM, D_MODEL), F32), pltpu.SemaphoreType.DMA((2,))],
        compiler_params=_cparams(1),
        name="moe_combine",
    )(d3, d3, x, mods, route, ys)


def _rope_tables(n_lat, d_rot, lane0):
    n_freq = d_rot // 4
    t = jnp.arange(n_lat, dtype=jnp.int32)
    row = (t // GRID_W).astype(F32)
    col = (t % GRID_W).astype(F32)
    inv = ROPE_BASE ** (-jnp.arange(n_freq, dtype=F32) / n_freq)
    ang = jnp.stack([row[:, None] * inv, col[:, None] * inv], axis=1)
    cos = jnp.broadcast_to(jnp.cos(ang)[:, :, None, :], (n_lat, 2, 2, n_freq)).reshape(n_lat, d_rot)
    sin = jnp.sin(ang)[:, :, None, :] * jnp.array([-1.0, 1.0], F32)[None, None, :, None]
    sin = sin.reshape(n_lat, d_rot)
    cos_full = jnp.ones((n_lat + TM, HEAD_LANES), F32).at[:n_lat, lane0:lane0 + d_rot].set(cos)
    sin_full = jnp.zeros((n_lat + TM, HEAD_LANES), F32).at[:n_lat, lane0:lane0 + d_rot].set(sin)
    return cos_full, sin_full


def _pad_heads(w, n_heads, width, lane0=0):
    lead = w.shape[:-1]
    w = w.reshape(*lead, n_heads, width)
    w = jnp.pad(w, [(0, 0)] * len(lead) + [(0, 0), (lane0, HEAD_LANES - lane0 - width)])
    return w.reshape(*lead, n_heads * HEAD_LANES)


def _pad_lanes(v, lane0=0):
    return jnp.pad(v, (lane0, HEAD_LANES - lane0 - v.shape[0])).reshape(1, HEAD_LANES)


def kernel(x, c, ctx, c_ctx, w_ada, b_ada, norm1_g, norm2_g, w_in_e, w_out_e, s5_a_re, s5_a_im, s5_log_dt, s5_b_re, s5_b_im, s5_c_re, s5_c_im, s5_d, s5_w_glu, mla_gq, mla_w_uq, mla_gkv, mla_w_ukv, mla_qn, mla_kn, w_qkv_o, w_out_o, gqa_qn, gqa_kn, moe_w_r1, moe_b_r1, moe_w_r2, moe_b_r2, moe_w_gate, moe_w_up, moe_w_down):
    batch, n_lat, _ = x.shape
    n_ctx = ctx.shape[1]
    depth = w_ada.shape[0]
    rows = _Rows(batch, n_lat, n_ctx, TM)
    n_lat_rows = batch * n_lat

    cond = jnp.zeros((8, D_MODEL), F32).at[:batch].set(c).at[batch].set(c_ctx)
    mods_all = _ada_table(cond, w_ada, b_ada).reshape(depth, 8, 1, N_MOD * D_MODEL)

    rope_gqa = _rope_tables(n_lat, GQA_DIM, 0)
    rope_mla = _rope_tables(n_lat, MLA_ROPE, MLA_NOPE)

    xs = jnp.concatenate([x.reshape(n_lat_rows, D_MODEL), ctx.reshape(batch * n_ctx, D_MODEL)], axis=0)

    for l in range(depth):
        last = l == depth - 1
        i = l // 2
        mods = mods_all[l]
        g1 = norm1_g[l].reshape(1, D_MODEL)
        g2 = norm2_g[l].reshape(1, D_MODEL)
        if l % 2 == 0:
            w_in = jnp.pad(w_in_e[i], ((0, 0), (0, D_MODEL - w_in_e.shape[2]))).astype(BF16)
            w_ukv = mla_w_ukv[i].reshape(MLA_KV_RANK, MLA_HEADS, MLA_NOPE + MLA_V)
            w_k = _pad_heads(w_ukv[..., :MLA_NOPE].reshape(MLA_KV_RANK, -1), MLA_HEADS, MLA_NOPE).astype(BF16)
            w_v = w_ukv[..., MLA_NOPE:].reshape(MLA_KV_RANK, -1).astype(BF16)
            u, q, k, vt = _in_even(
                xs, mods, g1, w_in, mla_gq[i].reshape(1, -1),
                _pad_heads(mla_w_uq[i], MLA_HEADS, MLA_QK).astype(BF16), mla_gkv[i].reshape(1, -1), w_k, w_v,
                _pad_lanes(mla_qn[i]), _pad_lanes(mla_kn[i]), *rope_mla, rows)
            o = _attention_all(q, k, vt, batch, n_lat, n_ctx, False, MLA_V, not last)
            s5p = _s5_params(s5_a_re[i], s5_a_im[i], s5_log_dt[i], s5_b_re[i], s5_b_im[i], s5_c_re[i], s5_c_im[i])
            y_f, y_r = _s5_mixer(u, s5p, batch, n_lat, n_ctx)
            mixer_inputs = (y_f, y_r, u, s5_d[i].reshape(1, -1), s5_w_glu[i].astype(BF16), o,
                            w_out_e[i][:S5_WIDTH].astype(BF16), w_out_e[i][S5_WIDTH:].astype(BF16))
        else:
            q, k, vt = _in_odd(xs, mods, g1, w_qkv_o[i].astype(BF16), gqa_qn[i].reshape(1, -1),
                               gqa_kn[i].reshape(1, -1), *rope_gqa, rows)
            o = _attention_all(q, k, vt, batch, n_lat, n_ctx, True, GQA_DIM, not last)
            mixer_inputs = (o, w_out_o[i].astype(BF16))

        w_route = jnp.pad(jnp.concatenate([moe_w_r1[l], moe_w_r2[l]], axis=1),
                          ((0, 0), (0, HEAD_LANES - MOE_GROUPS - MOE_EXPERTS)))
        b_route = _pad_lanes(jnp.concatenate([moe_b_r1[l], moe_b_r2[l]]))
        n_tiles = rows.lat_tiles if last else rows.all_tiles
        n_tok = n_tiles * TM
        xs, g, route = _out_proj(xs, mods, g2, mixer_inputs, w_route, b_route, rows, n_tiles, l % 2 == 0)
        src_tok, blk_e, n_used, dest = _moe_plan(route, n_tok)
        ys = _moe_ffn(g, src_tok, blk_e, n_used, moe_w_gate[l], moe_w_up[l], moe_w_down[l])
        xs = _combine(xs, mods, route, ys, dest, rows, n_tok)
    return xs[:n_lat_rows].reshape(batch, n_lat, D_MODEL)
```

```python
import functools
import math

import jax
import jax.numpy as jnp
import numpy as np
from jax import lax
from jax.experimental import pallas as pl
from jax.experimental.pallas import tpu as pltpu

F32 = jnp.float32
BF16 = jnp.bfloat16
HIGHEST = lax.Precision.HIGHEST

D_MODEL = 1024
N_MOD = 6
RMS_EPS = 1e-6
ROPE_BASE = 10000.0
GRID_W = 64

S5_WIDTH = 512
S5_GROUP = 16
S5_GROUPS = 32
S5_STATE = 64
S5_QUARTERS = 4
S5_GPQ = S5_GROUPS // S5_QUARTERS
S5_COLS = S5_GROUPS * S5_STATE

MLA_V = 64
MLA_HEADS = 8
MLA_NOPE = 64
MLA_ROPE = 32
MLA_QK = MLA_NOPE + MLA_ROPE
MLA_Q_RANK = 256
MLA_KV_RANK = 128

HEAD_LANES = 128
TOKEN_TILE = D_MODEL // HEAD_LANES
GQA_DIM = 128
GQA_HEADS = 8
GQA_KV_HEADS = 4

MOE_GROUPS = 4
MOE_PER_GROUP = 8
MOE_EXPERTS = 32
MOE_FF = 512

TM = 256
TQ = 256
PV_CHUNK = 256
V_ONES = 16
LOG2_E = math.log2(math.e)
S5_TC = 128
MOE_BM = 256
CMB_TM = 128
VMEM_LIMIT = 56 * 1024 * 1024
NEG_BIG = -1e30


def _cparams(n_axes):
    return pltpu.CompilerParams(dimension_semantics=("arbitrary",) * n_axes, vmem_limit_bytes=VMEM_LIMIT)


def _rms(x, width):
    return x * lax.rsqrt(jnp.sum(x * x, axis=-1, keepdims=True) * (1.0 / width) + RMS_EPS)


def _sigmoid(x):
    return 1.0 / (1.0 + jnp.exp(-x))


def _rope(x, cos, sin_signed, half):
    lane = lax.broadcasted_iota(jnp.int32, x.shape, 1)
    lo = (lane & (2 * half - 1)) < half
    partner = jnp.where(lo, pltpu.roll(x, HEAD_LANES - half, 1), pltpu.roll(x, half, 1))
    return x * cos + partner * sin_signed


def _ada_kernel(c_ref, w_ref, b_ref, o_ref):
    c = c_ref[...]
    sc = c * _sigmoid(c)
    o_ref[0] = jnp.dot(sc, w_ref[0], precision=HIGHEST, preferred_element_type=F32) + b_ref[0]


def _ada_table(cond, w_ada, b_ada):
    depth = w_ada.shape[0]
    n_out = w_ada.shape[2]
    tn = 1536
    return pl.pallas_call(
        _ada_kernel,
        grid=(depth, n_out // tn),
        in_specs=[
            pl.BlockSpec((8, D_MODEL), lambda l, j: (0, 0)),
            pl.BlockSpec((1, D_MODEL, tn), lambda l, j: (l, 0, j)),
            pl.BlockSpec((1, 1, tn), lambda l, j: (l, 0, j)),
        ],
        out_specs=pl.BlockSpec((1, 8, tn), lambda l, j: (l, 0, j)),
        out_shape=jax.ShapeDtypeStruct((depth, 8, n_out), F32),
        compiler_params=_cparams(2),
        name="ada_table",
    )(cond, w_ada, b_ada.reshape(depth, 1, n_out))


def _norm_modulate(x, gain, mod, k):
    shift = mod[:, k * D_MODEL:(k + 1) * D_MODEL]
    scale = mod[:, (k + 1) * D_MODEL:(k + 2) * D_MODEL]
    return _rms(x, D_MODEL) * gain * (1.0 + scale) + shift


class _Rows:
    def __init__(self, batch, n_lat, n_ctx, tile):
        assert n_lat % tile == 0 and n_ctx % tile == 0
        self.tile = tile
        self.lat_tiles = batch * n_lat // tile
        self.per_batch = n_lat // tile
        self.ctx_per_batch = n_ctx // tile
        self.all_tiles = self.lat_tiles + batch * n_ctx // tile
        self.batch = batch

    def mod_row(self, i):
        return jnp.where(i < self.lat_tiles, i // self.per_batch, self.batch)

    def rope_tile(self, i):
        return jnp.where(i < self.lat_tiles, i % self.per_batch, self.per_batch)


def _store_vt(vt_ref, v, n_heads, dv):
    vt = v.T
    vr = dv + V_ONES
    ones = jnp.ones((V_ONES, vt.shape[1]), BF16)
    for hd in range(n_heads):
        vt_ref[hd * vr:hd * vr + dv, :] = vt[hd * dv:(hd + 1) * dv].astype(BF16)
        vt_ref[hd * vr + dv:(hd + 1) * vr, :] = ones


def _in_odd_kernel(x_ref, mod_ref, g1_ref, w_ref, qn_ref, kn_ref, cos_ref, sin_ref, q_ref, k_ref, vt_ref):
    h = _norm_modulate(x_ref[...], g1_ref[...], mod_ref[0], 0)
    y = jnp.dot(h.astype(BF16), w_ref[...], preferred_element_type=F32)
    cos = cos_ref[...]
    sin = sin_ref[...]
    q_scale = GQA_DIM ** -0.5 * LOG2_E
    for hd in range(GQA_HEADS):
        qh = _rms(y[:, hd * GQA_DIM:(hd + 1) * GQA_DIM], GQA_DIM) * qn_ref[...]
        q_ref[:, hd * GQA_DIM:(hd + 1) * GQA_DIM] = (_rope(qh, cos, sin, GQA_DIM // 4) * q_scale).astype(BF16)
    k0 = GQA_HEADS * GQA_DIM
    for hd in range(GQA_KV_HEADS):
        kh = _rms(y[:, k0 + hd * GQA_DIM:k0 + (hd + 1) * GQA_DIM], GQA_DIM) * kn_ref[...]
        k_ref[:, hd * GQA_DIM:(hd + 1) * GQA_DIM] = _rope(kh, cos, sin, GQA_DIM // 4).astype(BF16)
    v0 = k0 + GQA_KV_HEADS * GQA_DIM
    _store_vt(vt_ref, y[:, v0:v0 + GQA_KV_HEADS * GQA_DIM], GQA_KV_HEADS, GQA_DIM)


def _in_odd(x, mods, g1, w_qkv, qn, kn, cos_tab, sin_tab, rows):
    n_rows = x.shape[0]
    n_q = GQA_HEADS * GQA_DIM
    n_kv = GQA_KV_HEADS * GQA_DIM
    vt_rows = GQA_KV_HEADS * (GQA_DIM + V_ONES)
    row = lambda i: (i, 0)
    fixed = lambda i: (0, 0)
    return pl.pallas_call(
        _in_odd_kernel,
        grid=(rows.all_tiles,),
        in_specs=[
            pl.BlockSpec((TM, D_MODEL), row),
            pl.BlockSpec((1, 1, N_MOD * D_MODEL), lambda i: (rows.mod_row(i), 0, 0)),
            pl.BlockSpec((1, D_MODEL), fixed),
            pl.BlockSpec((D_MODEL, n_q + 2 * n_kv), fixed),
            pl.BlockSpec((1, GQA_DIM), fixed),
            pl.BlockSpec((1, GQA_DIM), fixed),
            pl.BlockSpec((TM, HEAD_LANES), lambda i: (rows.rope_tile(i), 0)),
            pl.BlockSpec((TM, HEAD_LANES), lambda i: (rows.rope_tile(i), 0)),
        ],
        out_specs=[
            pl.BlockSpec((TM, n_q), row),
            pl.BlockSpec((TM, n_kv), row),
            pl.BlockSpec((vt_rows, TM), lambda i: (0, i)),
        ],
        out_shape=[
            jax.ShapeDtypeStruct((n_rows, n_q), BF16),
            jax.ShapeDtypeStruct((n_rows, n_kv), BF16),
            jax.ShapeDtypeStruct((vt_rows, n_rows), BF16),
        ],
        compiler_params=_cparams(1),
        name="in_proj_gqa",
    )(x, mods, g1, w_qkv, qn, kn, cos_tab, sin_tab)


def _in_even_kernel(x_ref, mod_ref, g1_ref, w_ref, gq_ref, wuq_ref, gkv_ref, wk_ref, wv_ref, qn_ref, kn_ref,
                    cos_ref, sin_ref, u_ref, q_ref, k_ref, vt_ref):
    h = _norm_modulate(x_ref[...], g1_ref[...], mod_ref[0], 0)
    y = jnp.dot(h.astype(BF16), w_ref[...], preferred_element_type=F32)
    u_ref[...] = y[:, :S5_WIDTH]
    c0 = S5_WIDTH
    cq = _rms(y[:, c0:c0 + MLA_Q_RANK], MLA_Q_RANK) * gq_ref[...]
    c1 = c0 + MLA_Q_RANK
    ckv = (_rms(y[:, c1:c1 + MLA_KV_RANK], MLA_KV_RANK) * gkv_ref[...]).astype(BF16)
    c2 = c1 + MLA_KV_RANK
    kr = pltpu.roll(y[:, c2:c2 + HEAD_LANES], MLA_NOPE, 1)
    q_all = jnp.dot(cq.astype(BF16), wuq_ref[...], preferred_element_type=F32)
    k_all = jnp.dot(ckv, wk_ref[...], preferred_element_type=F32)
    _store_vt(vt_ref, jnp.dot(ckv, wv_ref[...], preferred_element_type=F32), MLA_HEADS, MLA_V)
    cos = cos_ref[...]
    sin = sin_ref[...]
    lane = lax.broadcasted_iota(jnp.int32, kr.shape, 1)
    q_scale = MLA_QK ** -0.5 * LOG2_E
    for hd in range(MLA_HEADS):
        sl = slice(hd * HEAD_LANES, (hd + 1) * HEAD_LANES)
        qh = _rms(q_all[:, sl], MLA_QK) * qn_ref[...]
        q_ref[:, sl] = (_rope(qh, cos, sin, MLA_ROPE // 4) * q_scale).astype(BF16)
        kh = jnp.where(lane < MLA_NOPE, k_all[:, sl], kr)
        kh = _rms(kh, MLA_QK) * kn_ref[...]
        k_ref[:, sl] = _rope(kh, cos, sin, MLA_ROPE // 4).astype(BF16)


def _in_even(x, mods, g1, w_in, gq, w_uq, gkv, w_k, w_v, qn, kn, cos_tab, sin_tab, rows):
    n_rows = x.shape[0]
    n_h = MLA_HEADS * HEAD_LANES
    vt_rows = MLA_HEADS * (MLA_V + V_ONES)
    row = lambda i: (i, 0)
    fixed = lambda i: (0, 0)
    return pl.pallas_call(
        _in_even_kernel,
        grid=(rows.all_tiles,),
        in_specs=[
            pl.BlockSpec((TM, D_MODEL), row),
            pl.BlockSpec((1, 1, N_MOD * D_MODEL), lambda i: (rows.mod_row(i), 0, 0)),
            pl.BlockSpec((1, D_MODEL), fixed),
            pl.BlockSpec((D_MODEL, D_MODEL), fixed),
            pl.BlockSpec((1, MLA_Q_RANK), fixed),
            pl.BlockSpec((MLA_Q_RANK, n_h), fixed),
            pl.BlockSpec((1, MLA_KV_RANK), fixed),
            pl.BlockSpec((MLA_KV_RANK, n_h), fixed),
            pl.BlockSpec((MLA_KV_RANK, MLA_HEADS * MLA_V), fixed),
            pl.BlockSpec((1, HEAD_LANES), fixed),
            pl.BlockSpec((1, HEAD_LANES), fixed),
            pl.BlockSpec((TM, HEAD_LANES), lambda i: (rows.rope_tile(i), 0)),
            pl.BlockSpec((TM, HEAD_LANES), lambda i: (rows.rope_tile(i), 0)),
        ],
        out_specs=[
            pl.BlockSpec((TM, S5_WIDTH), row),
            pl.BlockSpec((TM, n_h), row),
            pl.BlockSpec((TM, n_h), row),
            pl.BlockSpec((vt_rows, TM), lambda i: (0, i)),
        ],
        out_shape=[
            jax.ShapeDtypeStruct((n_rows, S5_WIDTH), F32),
            jax.ShapeDtypeStruct((n_rows, n_h), BF16),
            jax.ShapeDtypeStruct((n_rows, n_h), BF16),
            jax.ShapeDtypeStruct((vt_rows, n_rows), BF16),
        ],
        compiler_params=_cparams(1),
        name="in_proj_s5_mla",
    )(x, mods, g1, w_in, gq, w_uq, gkv, w_k, w_v, qn, kn, cos_tab, sin_tab)


def _attn_kernel(*refs, n_seg, units, dv):
    q_ref = refs[0]
    o_ref = refs[-1]
    segs = [(refs[1 + 2 * s], refs[2 + 2 * s]) for s in range(n_seg)]
    nt = (((1,), (1,)), ((), ()))
    vr = dv + V_ONES
    tq = q_ref.shape[0]
    outs = []
    for q_blks, k_blk, v_blk in units:
        q = jnp.concatenate([q_ref[:, b * HEAD_LANES:(b + 1) * HEAD_LANES] for b in q_blks], axis=0)
        scores = [lax.dot_general(k_ref[:, k_blk * HEAD_LANES:(k_blk + 1) * HEAD_LANES], q, nt,
                                  preferred_element_type=F32) for k_ref, _ in segs]
        m = functools.reduce(jnp.maximum, [jnp.max(s, axis=0, keepdims=True) for s in scores])
        acc = None
        for s, (_, vt_ref) in zip(scores, segs):
            for c in range(0, s.shape[0], PV_CHUNK):
                p = jnp.exp2(s[c:c + PV_CHUNK] - m).astype(BF16)
                part = jnp.dot(vt_ref[v_blk * vr:(v_blk + 1) * vr, c:c + PV_CHUNK], p, preferred_element_type=F32)
                acc = part if acc is None else acc + part
        o_t = acc[:dv] / acc[dv:dv + 1]
        outs += [o_t[:, j * tq:(j + 1) * tq] for j in range(len(q_blks))]
    o_ref[...] = jnp.concatenate(outs, axis=0).T.astype(BF16)


def _attention(q, k, vt, batch, n_lat, n_ctx, shared_kv, dv, latent_queries):
    ctx_blk0 = batch * n_lat // n_ctx
    n_steps = q.shape[1] // (2 * HEAD_LANES)
    k_width = HEAD_LANES if shared_kv else 2 * HEAD_LANES
    vt_rows = (dv + V_ONES) * (1 if shared_kv else 2)
    units = (((0, 1), 0, 0),) if shared_kv else (((0,), 0, 0), ((1,), 1, 1))
    tq_lat = TQ if shared_kv else 2 * TQ
    ctx_k = pl.BlockSpec((n_ctx, k_width), lambda b, h, i: (ctx_blk0 + b, h))
    ctx_v = pl.BlockSpec((vt_rows, n_ctx), lambda b, h, i: (h, ctx_blk0 + b))
    if latent_queries:
        tq, q_tiles = tq_lat, n_lat // tq_lat
        q_spec = pl.BlockSpec((tq, 2 * HEAD_LANES), lambda b, h, i: (b * q_tiles + i, h))
        o_spec = pl.BlockSpec((tq, 2 * dv), lambda b, h, i: (b * q_tiles + i, h))
        kv_specs = [pl.BlockSpec((n_lat, k_width), lambda b, h, i: (b, h)),
                    pl.BlockSpec((vt_rows, n_lat), lambda b, h, i: (h, b)), ctx_k, ctx_v]
        kv_args = (k, vt, k, vt)
        out_rows = batch * n_lat
    else:
        tq, q_tiles = n_ctx, 1
        q_spec = pl.BlockSpec((tq, 2 * HEAD_LANES), lambda b, h, i: (ctx_blk0 + b, h))
        o_spec = pl.BlockSpec((tq, 2 * dv), lambda b, h, i: (b, h))
        kv_specs = [ctx_k, ctx_v]
        kv_args = (k, vt)
        out_rows = batch * n_ctx
    return pl.pallas_call(
        functools.partial(_attn_kernel, n_seg=len(kv_args) // 2, units=units, dv=dv),
        grid=(batch, n_steps, q_tiles),
        in_specs=[q_spec] + kv_specs,
        out_specs=o_spec,
        out_shape=jax.ShapeDtypeStruct((out_rows, n_steps * 2 * dv), BF16),
        compiler_params=_cparams(3),
        name="attention_lat" if latent_queries else "attention_ctx",
    )(q, *kv_args)


def _attention_all(q, k, vt, batch, n_lat, n_ctx, shared_kv, dv, need_ctx):
    o = _attention(q, k, vt, batch, n_lat, n_ctx, shared_kv, dv, True)
    if need_ctx:
        o = jnp.concatenate([o, _attention(q, k, vt, batch, n_lat, n_ctx, shared_kv, dv, False)], axis=0)
    return o


def _s5_kernel(ulat_f, uctx_f, ulat_r, uctx_r, flip_ref, wdr_ref, wdi_ref, wc_ref, ar_ref, ai_ref,
               yf_lat, yf_ctx, yr_lat, yr_ctx, u_ref, y_ref, bur, bui, hr, hi, *, ctx_chunks):
    i = pl.program_id(0)
    batch, tc = ulat_f.shape[0], ulat_f.shape[1]

    @pl.when(i == 0)
    def _():
        hr[...] = jnp.zeros_like(hr)
        hi[...] = jnp.zeros_like(hi)

    qw = S5_GPQ * S5_GROUP
    qs = S5_GPQ * S5_STATE

    def load(src_f, src_r):
        for b in range(batch):
            u_f = src_f[b]
            u_r = jnp.dot(flip_ref[...], src_r[b].astype(BF16), preferred_element_type=F32)
            for q in range(S5_QUARTERS):
                u_ref[q, pl.ds(b, tc, stride=8), :] = u_f[:, q * qw:(q + 1) * qw]
                u_ref[q, pl.ds(batch + b, tc, stride=8), :] = u_r[:, q * qw:(q + 1) * qw]

    @pl.when(i < ctx_chunks)
    def _():
        load(uctx_f, uctx_r)

    @pl.when(i >= ctx_chunks)
    def _():
        load(ulat_f, ulat_r)

    n_rows = u_ref.shape[1]
    fwd = (lax.broadcasted_iota(jnp.int32, (n_rows, qw), 0) & 7) < 4
    for q in range(S5_QUARTERS):
        uq = u_ref[q]
        lhs = jnp.concatenate([jnp.where(fwd, uq, 0.0), jnp.where(fwd, 0.0, uq)], axis=1).astype(BF16)
        bur[:, q * qs:(q + 1) * qs] = jnp.dot(lhs, wdr_ref[q], preferred_element_type=F32)
        bui[:, q * qs:(q + 1) * qs] = jnp.dot(lhs, wdi_ref[q], preferred_element_type=F32)

    steps = n_rows // 8
    for cb in range(S5_COLS // qs):
        cols = slice(cb * qs, (cb + 1) * qs)
        a_r = ar_ref[:, cols]
        a_i = ai_ref[:, cols]

        def step(t, carry, cols=cols, a_r=a_r, a_i=a_i):
            h_r, h_i = carry
            r0 = pl.multiple_of(t * 8, 8)
            n_r = a_r * h_r - a_i * h_i + bur[pl.ds(r0, 8), cols]
            n_i = a_r * h_i + a_i * h_r + bui[pl.ds(r0, 8), cols]
            bur[pl.ds(r0, 8), cols] = n_r
            bui[pl.ds(r0, 8), cols] = n_i
            return n_r, n_i

        h_r, h_i = lax.fori_loop(0, steps, step, (hr[:, cols], hi[:, cols]), unroll=8)
        hr[:, cols] = h_r
        hi[:, cols] = h_i

    for q in range(S5_QUARTERS):
        hq = jnp.concatenate([bur[:, q * qs:(q + 1) * qs], bui[:, q * qs:(q + 1) * qs]], axis=1).astype(BF16)
        o = jnp.dot(hq, wc_ref[q], preferred_element_type=F32)
        y_ref[q] = jnp.where(fwd, o[:, :qw], o[:, qw:])

    def store(dst_f, dst_r):
        gather = lambda s: jnp.concatenate([y_ref[q, pl.ds(s, tc, stride=8), :] for q in range(S5_QUARTERS)], axis=1)
        for b in range(batch):
            dst_f[b] = gather(b)
            y_rev = gather(batch + b)
            top = y_rev.astype(BF16)
            rest = (y_rev - top.astype(F32)).astype(BF16)
            dst_r[b] = (jnp.dot(flip_ref[...], top, preferred_element_type=F32)
                        + jnp.dot(flip_ref[...], rest, preferred_element_type=F32))

    @pl.when(i < ctx_chunks)
    def _():
        store(yf_ctx, yr_ctx)

    @pl.when(i >= ctx_chunks)
    def _():
        store(yf_lat, yr_lat)


def _s5_scan(u_lat, u_ctx, wdr, wdi, wc, a_r, a_i):
    batch, n_lat, _ = u_lat.shape
    n_ctx = u_ctx.shape[1]
    tc = S5_TC
    assert 2 * batch == 8 and n_lat % tc == 0 and n_ctx % tc == 0
    nc, nl = n_ctx // tc, n_lat // tc
    blk = tc * 8
    qs = S5_GPQ * S5_STATE
    qw = S5_GPQ * S5_GROUP
    flip = jnp.eye(tc, dtype=BF16)[::-1]
    lat_f = lambda i: (0, jnp.clip(i - nc, 0, nl - 1), 0)
    ctx_f = lambda i: (0, jnp.clip(i, 0, nc - 1), 0)
    lat_r = lambda i: (0, jnp.clip(nl - 1 - (i - nc), 0, nl - 1), 0)
    ctx_r = lambda i: (0, jnp.clip(nc - 1 - i, 0, nc - 1), 0)
    tile = (batch, tc, S5_WIDTH)
    whole = lambda shape: pl.BlockSpec(shape, lambda i: (0,) * len(shape))
    return pl.pallas_call(
        functools.partial(_s5_kernel, ctx_chunks=nc),
        grid=(nc + nl,),
        in_specs=[
            pl.BlockSpec(tile, lat_f), pl.BlockSpec(tile, ctx_f), pl.BlockSpec(tile, lat_r), pl.BlockSpec(tile, ctx_r),
            whole((tc, tc)),
            whole((S5_QUARTERS, 2 * qw, qs)), whole((S5_QUARTERS, 2 * qw, qs)),
            whole((S5_QUARTERS, 2 * qs, 2 * qw)),
            whole((8, S5_COLS)), whole((8, S5_COLS)),
        ],
        out_specs=[pl.BlockSpec(tile, lat_f), pl.BlockSpec(tile, ctx_f), pl.BlockSpec(tile, lat_r),
                   pl.BlockSpec(tile, ctx_r)],
        out_shape=[jax.ShapeDtypeStruct(u_lat.shape, F32), jax.ShapeDtypeStruct(u_ctx.shape, F32),
                   jax.ShapeDtypeStruct(u_lat.shape, F32), jax.ShapeDtypeStruct(u_ctx.shape, F32)],
        scratch_shapes=[
            pltpu.VMEM((S5_QUARTERS, blk, qw), F32),
            pltpu.VMEM((S5_QUARTERS, blk, qw), F32),
            pltpu.VMEM((blk, S5_COLS), F32),
            pltpu.VMEM((blk, S5_COLS), F32),
            pltpu.VMEM((8, S5_COLS), F32),
            pltpu.VMEM((8, S5_COLS), F32),
        ],
        compiler_params=_cparams(1),
        name="s5_scan",
    )(u_lat, u_ctx, u_lat, u_ctx, flip, wdr, wdi, wc, a_r, a_i)


def _s5_params(a_re, a_im, log_dt, b_re, b_im, c_re, c_im):
    dt = jnp.exp(log_dt)[..., None]
    mag = jnp.exp(dt * a_re)
    th = dt * a_im
    ab_r, ab_i = mag * jnp.cos(th), mag * jnp.sin(th)
    den = a_re * a_re + a_im * a_im
    nr = ab_r - 1
    z_r = (nr * a_re + ab_i * a_im) / den
    z_i = (ab_i * a_re - nr * a_im) / den
    bb_r = z_r[..., None] * b_re - z_i[..., None] * b_im
    bb_i = z_r[..., None] * b_im + z_i[..., None] * b_re
    eye = jnp.eye(S5_GPQ, dtype=F32)
    qs = S5_GPQ * S5_STATE
    qw = S5_GPQ * S5_GROUP

    def drive(bb):
        bq = bb.reshape(2, S5_QUARTERS, S5_GPQ, S5_STATE, S5_GROUP)
        return jnp.einsum('dqgnp,gh->qdgphn', bq, eye).reshape(S5_QUARTERS, 2 * qw, qs).astype(BF16)

    def readout(c):
        cq = c.reshape(2, S5_QUARTERS, S5_GPQ, S5_GROUP, S5_STATE)
        return jnp.einsum('dqgpn,gh->qgndhp', cq, eye).reshape(S5_QUARTERS, qs, 2 * qw)

    wc = jnp.concatenate([readout(c_re), -readout(c_im)], axis=1).astype(BF16)
    expand = lambda a: jnp.repeat(a.reshape(2, S5_COLS), 4, axis=0)
    return drive(bb_r), drive(bb_i), wc, expand(ab_r), expand(ab_i)


def _s5_mixer(u, params, batch, n_lat, n_ctx):
    u_lat = u[:batch * n_lat].reshape(batch, n_lat, S5_WIDTH)
    u_ctx = u[batch * n_lat:].reshape(batch, n_ctx, S5_WIDTH)
    yf_lat, yf_ctx, yr_lat, yr_ctx = _s5_scan(u_lat, u_ctx, *params)
    to_rows = lambda lat, ctx: jnp.concatenate(
        [lat.reshape(batch * n_lat, S5_WIDTH), ctx.reshape(batch * n_ctx, S5_WIDTH)], axis=0)
    return to_rows(yf_lat, yf_ctx), to_rows(yr_lat, yr_ctx)


def _route(logits):
    lane = lax.broadcasted_iota(jnp.int32, logits.shape, 1).astype(F32)
    first = lambda hit: jnp.min(jnp.where(hit, lane, float(HEAD_LANES)), axis=-1, keepdims=True)
    is_grp = lane < MOE_GROUPS
    l1 = jnp.where(is_grp, logits, NEG_BIG)
    m1 = jnp.max(l1, axis=-1, keepdims=True)
    grp = first(l1 == m1)
    p_grp = 1.0 / jnp.sum(jnp.where(is_grp, jnp.exp(l1 - m1), 0.0), axis=-1, keepdims=True)
    lo = MOE_GROUPS + grp * MOE_PER_GROUP
    l2 = jnp.where((lane >= lo) & (lane < lo + MOE_PER_GROUP), logits, NEG_BIG)
    v1 = jnp.max(l2, axis=-1, keepdims=True)
    i1 = first(l2 == v1)
    l2 = jnp.where(lane == i1, NEG_BIG, l2)
    v2 = jnp.max(l2, axis=-1, keepdims=True)
    i2 = first(l2 == v2)
    e2 = jnp.exp(v2 - v1)
    g1 = p_grp / (1.0 + e2)
    g2 = p_grp * e2 / (1.0 + e2)
    return i1 - MOE_GROUPS, i2 - MOE_GROUPS, g1, g2


def _expert_ranks(ex1, ex2, count_ref):
    n = ex1.shape[0]
    lane = lax.broadcasted_iota(jnp.int32, (n, HEAD_LANES), 1).astype(F32)
    tri = (lax.broadcasted_iota(jnp.int32, (n, n), 0) >= lax.broadcasted_iota(jnp.int32, (n, n), 1))
    tri = jnp.where(tri, 1.0, 0.0).astype(BF16)
    hit1 = lane == ex1
    hit2 = lane == ex2
    cum1 = jnp.dot(tri, jnp.where(hit1, 1.0, 0.0).astype(BF16), preferred_element_type=F32)
    cum2 = jnp.dot(tri, jnp.where(hit2, 1.0, 0.0).astype(BF16), preferred_element_type=F32)
    before = count_ref[0:1, :]
    tot1 = cum1[n - 1:n, :]
    rank1 = jnp.sum(jnp.where(hit1, before + cum1 - 1.0, 0.0), axis=-1, keepdims=True)
    rank2 = jnp.sum(jnp.where(hit2, before + tot1 + cum2 - 1.0, 0.0), axis=-1, keepdims=True)
    count_ref[...] = jnp.broadcast_to(before + tot1 + cum2[n - 1:n, :], count_ref.shape)
    return rank1, rank2


def _out_kernel(*refs, even):
    if even:
        (x_ref, mod_ref, g2_ref, yf_ref, yr_ref, u_ref, d_ref, wglu_ref, o_ref, wa_ref, wb_ref, wr_ref, br_ref,
         xo_ref, g_ref, rt_ref, cnt_ref) = refs
        u = u_ref[...]
        z = yf_ref[...] + yr_ref[...] + d_ref[...] * u
        gz = 0.5 * z * (1.0 + jnp.tanh(math.sqrt(2.0 / math.pi) * (z + 0.044715 * (z * z * z))))
        s5 = gz * _sigmoid(jnp.dot(gz.astype(BF16), wglu_ref[...], preferred_element_type=F32))
        mix = (jnp.dot(s5.astype(BF16), wa_ref[...], preferred_element_type=F32)
               + jnp.dot(o_ref[...], wb_ref[...], preferred_element_type=F32))
    else:
        x_ref, mod_ref, g2_ref, o_ref, wb_ref, wr_ref, br_ref, xo_ref, g_ref, rt_ref, cnt_ref = refs
        mix = jnp.dot(o_ref[...], wb_ref[...], preferred_element_type=F32)

    @pl.when(pl.program_id(0) == 0)
    def _():
        cnt_ref[...] = jnp.zeros_like(cnt_ref)

    mod = mod_ref[0]
    x = x_ref[...] + mod[:, 2 * D_MODEL:3 * D_MODEL] * mix
    xo_ref[...] = x
    g = _norm_modulate(x, g2_ref[...], mod, 3)
    _store_token_tiles(g_ref, g)
    logits = jnp.dot(g, wr_ref[...], precision=HIGHEST, preferred_element_type=F32) + br_ref[...]
    ex1, ex2, gate1, gate2 = _route(logits)
    rank1, rank2 = _expert_ranks(ex1, ex2, cnt_ref)
    lane = lax.broadcasted_iota(jnp.int32, logits.shape, 1)
    out = jnp.zeros_like(logits)
    for k, col in enumerate((ex1, ex2, gate1, gate2, rank1, rank2)):
        out = jnp.where(lane == k, col, out)
    rt_ref[...] = out


def _out_proj(x, mods, g2, mixer_inputs, w_route, b_route, rows, n_tiles, even):
    row = lambda i: (i, 0)
    fixed = lambda i: (0, 0)
    n_rows = n_tiles * TM
    head_specs = [
        pl.BlockSpec((TM, D_MODEL), row),
        pl.BlockSpec((1, 1, N_MOD * D_MODEL), lambda i: (rows.mod_row(i), 0, 0)),
        pl.BlockSpec((1, D_MODEL), fixed),
    ]
    if even:
        mix_specs = [
            pl.BlockSpec((TM, S5_WIDTH), row), pl.BlockSpec((TM, S5_WIDTH), row), pl.BlockSpec((TM, S5_WIDTH), row),
            pl.BlockSpec((1, S5_WIDTH), fixed), pl.BlockSpec((S5_WIDTH, S5_WIDTH), fixed),
            pl.BlockSpec((TM, MLA_HEADS * MLA_V), row),
            pl.BlockSpec((S5_WIDTH, D_MODEL), fixed), pl.BlockSpec((MLA_HEADS * MLA_V, D_MODEL), fixed),
        ]
    else:
        mix_specs = [pl.BlockSpec((TM, D_MODEL), row), pl.BlockSpec((D_MODEL, D_MODEL), fixed)]
    tail_specs = [pl.BlockSpec((D_MODEL, HEAD_LANES), fixed), pl.BlockSpec((1, HEAD_LANES), fixed)]
    return pl.pallas_call(
        functools.partial(_out_kernel, even=even),
        grid=(n_tiles,),
        in_specs=head_specs + mix_specs + tail_specs,
        out_specs=[pl.BlockSpec((TM, D_MODEL), row), pl.BlockSpec((TM * TOKEN_TILE, HEAD_LANES), row),
                   pl.BlockSpec((TM, HEAD_LANES), row), pl.BlockSpec((8, HEAD_LANES), fixed)],
        out_shape=[jax.ShapeDtypeStruct((n_rows, D_MODEL), F32),
                   jax.ShapeDtypeStruct((n_rows * TOKEN_TILE, HEAD_LANES), F32),
                   jax.ShapeDtypeStruct((n_rows, HEAD_LANES), F32), jax.ShapeDtypeStruct((8, HEAD_LANES), F32)],
        compiler_params=_cparams(1),
        name="out_proj_even" if even else "out_proj_odd",
    )(x, mods, g2, *mixer_inputs, w_route, b_route)


def _store_token_tiles(ref, value):
    n = value.shape[0]
    for s in range(TOKEN_TILE):
        ref[pl.ds(s, n, stride=TOKEN_TILE), :] = value[:, s * HEAD_LANES:(s + 1) * HEAD_LANES]


def _load_token_tiles(ref, first_token, n):
    return jnp.concatenate([ref[pl.ds(first_token * TOKEN_TILE + s, n, stride=TOKEN_TILE), :]
                            for s in range(TOKEN_TILE)], axis=1)


def _token_copy(src, src_token, dst, dst_token, sem):
    return pltpu.make_async_copy(src.at[pl.ds(pl.multiple_of(src_token * TOKEN_TILE, TOKEN_TILE), TOKEN_TILE), :],
                                 dst.at[pl.ds(pl.multiple_of(dst_token * TOKEN_TILE, TOKEN_TILE), TOKEN_TILE), :],
                                 sem)


def _dispatch_kernel(pad_lo, pad_hi, dst_ref, g_ref, xs_hbm, zero, sem):
    n_tok = g_ref.shape[0] // TOKEN_TILE

    def send(t, c):
        _token_copy(g_ref, t, xs_hbm, dst_ref[0, 0, t], sem).start()
        _token_copy(g_ref, t, xs_hbm, dst_ref[0, 0, n_tok + t], sem).start()
        return c

    lax.fori_loop(0, n_tok, send, 0, unroll=8)

    def drain(t, c):
        _token_copy(g_ref, 0, xs_hbm, 0, sem).wait()
        _token_copy(g_ref, 0, xs_hbm, 0, sem).wait()
        return c

    lax.fori_loop(0, n_tok, drain, 0, unroll=8)

    @pl.when(pl.program_id(0) == pl.num_programs(0) - 1)
    def _():
        zero[...] = jnp.zeros_like(zero)
        for e in range(MOE_EXPERTS):
            def fill(p, c):
                _token_copy(zero, 0, xs_hbm, p, sem).start()
                return c

            def fill_done(p, c):
                _token_copy(zero, 0, xs_hbm, p, sem).wait()
                return c

            lax.fori_loop(pad_lo[e], pad_hi[e], fill, 0)
            lax.fori_loop(pad_lo[e], pad_hi[e], fill_done, 0)

        blk_rows = zero.shape[0]
        n_blocks = xs_hbm.shape[0] // blk_rows
        first_free = pad_hi[MOE_EXPERTS - 1] * TOKEN_TILE // blk_rows
        block_copy = lambda b: pltpu.make_async_copy(
            zero, xs_hbm.at[pl.ds(pl.multiple_of(b * blk_rows, blk_rows), blk_rows), :], sem)

        def clear(b, c):
            block_copy(b).start()
            return c

        def clear_done(b, c):
            block_copy(b).wait()
            return c

        lax.fori_loop(first_free, n_blocks, clear, 0)
        lax.fori_loop(first_free, n_blocks, clear_done, 0)


def _dispatch(g_tiles, dest, pad_lo, pad_hi, n_tok, n_slots):
    n_tiles = n_tok // TM
    d3 = dest.reshape(n_tiles, TM, 2).transpose(0, 2, 1).reshape(n_tiles, 1, 2 * TM)
    grid_spec = pltpu.PrefetchScalarGridSpec(
        num_scalar_prefetch=2,
        grid=(n_tiles,),
        in_specs=[
            pl.BlockSpec((1, 1, 2 * TM), lambda i, lo, hi: (i, 0, 0), memory_space=pltpu.SMEM),
            pl.BlockSpec((TM * TOKEN_TILE, HEAD_LANES), lambda i, lo, hi: (i, 0)),
        ],
        out_specs=pl.BlockSpec(memory_space=pl.ANY),
        scratch_shapes=[pltpu.VMEM((MOE_BM * TOKEN_TILE, HEAD_LANES), F32), pltpu.SemaphoreType.DMA(())],
    )
    return pl.pallas_call(
        _dispatch_kernel,
        grid_spec=grid_spec,
        out_shape=jax.ShapeDtypeStruct((n_slots * TOKEN_TILE, HEAD_LANES), F32),
        compiler_params=_cparams(1),
        name="moe_dispatch",
    )(pad_lo, pad_hi, d3, g_tiles)


def _moe_kernel(blk_e, n_used, xs_ref, wg_ref, wu_ref, wd_ref, y_ref, wg_b, wu_b, wd_b):
    i = pl.program_id(0)

    @pl.when((i == 0) | (blk_e[i] != blk_e[jnp.maximum(i - 1, 0)]))
    def _():
        wg_b[...] = wg_ref[0, 0].astype(BF16)
        wu_b[...] = wu_ref[0, 0].astype(BF16)
        wd_b[...] = wd_ref[0, 0].astype(BF16)

    @pl.when(i < n_used[0])
    def _():
        x = _load_token_tiles(xs_ref, 0, MOE_BM).astype(BF16)
        a = jnp.dot(x, wg_b[...], preferred_element_type=F32)
        b = jnp.dot(x, wu_b[...], preferred_element_type=F32)
        hmid = (a * _sigmoid(a) * b).astype(BF16)
        _store_token_tiles(y_ref, jnp.dot(hmid, wd_b[...], preferred_element_type=F32))

    @pl.when(i >= n_used[0])
    def _():
        y_ref[...] = jnp.zeros_like(y_ref)


def _moe_ffn(xs_tiles, blk_e, n_used, w_gate, w_up, w_down, layer):
    n_blocks = blk_e.shape[0]
    blk = lambda i, be, nu: (jnp.minimum(i, nu[0] - 1), 0)
    expert = lambda i, be, nu: (layer, be[i], 0, 0)
    grid_spec = pltpu.PrefetchScalarGridSpec(
        num_scalar_prefetch=2,
        grid=(n_blocks,),
        in_specs=[
            pl.BlockSpec((MOE_BM * TOKEN_TILE, HEAD_LANES), blk),
            pl.BlockSpec((1, 1, D_MODEL, MOE_FF), expert),
            pl.BlockSpec((1, 1, D_MODEL, MOE_FF), expert),
            pl.BlockSpec((1, 1, MOE_FF, D_MODEL), expert),
        ],
        out_specs=pl.BlockSpec((MOE_BM * TOKEN_TILE, HEAD_LANES), lambda i, be, nu: (i, 0)),
        scratch_shapes=[
            pltpu.VMEM((D_MODEL, MOE_FF), BF16),
            pltpu.VMEM((D_MODEL, MOE_FF), BF16),
            pltpu.VMEM((MOE_FF, D_MODEL), BF16),
        ],
    )
    return pl.pallas_call(
        _moe_kernel,
        grid_spec=grid_spec,
        out_shape=jax.ShapeDtypeStruct(xs_tiles.shape, F32),
        compiler_params=_cparams(1),
        name="moe_ffn",
    )(blk_e, n_used, xs_tiles, w_gate, w_up, w_down)


def _moe_plan(route, counts, n_tok):
    expert = route[:n_tok, 0:2].astype(jnp.int32)
    rank = route[:n_tok, 4:6].astype(jnp.int32)
    counts = counts[0, :MOE_EXPERTS].astype(jnp.int32)
    n_blocks = (2 * n_tok + MOE_EXPERTS * (MOE_BM - 1) + MOE_BM - 1) // MOE_BM
    padded = (counts + MOE_BM - 1) // MOE_BM * MOE_BM
    pad_end = jnp.cumsum(padded)
    start = pad_end - padded
    dest = start[expert] + rank
    blk_start = jnp.arange(n_blocks, dtype=jnp.int32) * MOE_BM
    blk_e = jnp.minimum(jnp.sum((pad_end[None, :] <= blk_start[:, None]).astype(jnp.int32), axis=1),
                        MOE_EXPERTS - 1)
    n_used = (pad_end[-1:] // MOE_BM).astype(jnp.int32)
    return dest, blk_e, n_used, start + counts, pad_end, n_blocks * MOE_BM


def _combine_kernel(dst_ref, dst_next_ref, x_ref, mod_ref, rt_ref, ys_hbm, o_ref, buf, sem):
    i = pl.program_id(0)
    n = pl.num_programs(0)
    slot = i % 2
    rows = 2 * CMB_TM

    def gather(idx_ref, s):
        def issue(j, c):
            _token_copy(ys_hbm, idx_ref[0, 0, j], buf.at[s], j, sem.at[s]).start()
            return c
        lax.fori_loop(0, rows, issue, 0, unroll=8)

    @pl.when(i == 0)
    def _():
        gather(dst_ref, 0)

    @pl.when(i + 1 < n)
    def _():
        gather(dst_next_ref, 1 - slot)

    def drain(j, c):
        _token_copy(ys_hbm, 0, buf.at[slot], 0, sem.at[slot]).wait()
        return c

    lax.fori_loop(0, rows, drain, 0, unroll=8)
    rt = rt_ref[...]
    f = (rt[:, 2:3] * _load_token_tiles(buf.at[slot], 0, CMB_TM)
         + rt[:, 3:4] * _load_token_tiles(buf.at[slot], CMB_TM, CMB_TM))
    o_ref[...] = x_ref[...] + mod_ref[0][:, 5 * D_MODEL:6 * D_MODEL] * f


def _combine(x, mods, route, ys_tiles, dest, rows, n_tok):
    n_tiles = n_tok // CMB_TM
    d3 = dest.reshape(n_tiles, CMB_TM, 2).transpose(0, 2, 1).reshape(n_tiles, 1, 2 * CMB_TM)
    last = n_tiles - 1
    sub = TM // CMB_TM
    return pl.pallas_call(
        _combine_kernel,
        grid=(n_tiles,),
        in_specs=[
            pl.BlockSpec((1, 1, 2 * CMB_TM), lambda i: (i, 0, 0), memory_space=pltpu.SMEM),
            pl.BlockSpec((1, 1, 2 * CMB_TM), lambda i: (jnp.minimum(i + 1, last), 0, 0), memory_space=pltpu.SMEM),
            pl.BlockSpec((CMB_TM, D_MODEL), lambda i: (i, 0)),
            pl.BlockSpec((1, 1, N_MOD * D_MODEL), lambda i: (rows.mod_row(i // sub), 0, 0)),
            pl.BlockSpec((CMB_TM, HEAD_LANES), lambda i: (i, 0)),
            pl.BlockSpec(memory_space=pl.ANY),
        ],
        out_specs=pl.BlockSpec((CMB_TM, D_MODEL), lambda i: (i, 0)),
        out_shape=jax.ShapeDtypeStruct((n_tok, D_MODEL), F32),
        scratch_shapes=[pltpu.VMEM((2, 2 * CMB_TM * TOKEN_TILE, HEAD_LANES), F32), pltpu.SemaphoreType.DMA((2,))],
        compiler_params=_cparams(1),
        name="moe_combine",
    )(d3, d3, x, mods, route, ys_tiles)


def _rope_tables(n_lat, d_rot, lane0):
    n_freq = d_rot // 4
    t = jnp.arange(n_lat, dtype=jnp.int32)
    row = (t // GRID_W).astype(F32)
    col = (t % GRID_W).astype(F32)
    inv = ROPE_BASE ** (-jnp.arange(n_freq, dtype=F32) / n_freq)
    ang = jnp.stack([row[:, None] * inv, col[:, None] * inv], axis=1)
    cos = jnp.broadcast_to(jnp.cos(ang)[:, :, None, :], (n_lat, 2, 2, n_freq)).reshape(n_lat, d_rot)
    sin = jnp.sin(ang)[:, :, None, :] * jnp.array([-1.0, 1.0], F32)[None, None, :, None]
    sin = sin.reshape(n_lat, d_rot)
    cos_full = jnp.ones((n_lat + TM, HEAD_LANES), F32).at[:n_lat, lane0:lane0 + d_rot].set(cos)
    sin_full = jnp.zeros((n_lat + TM, HEAD_LANES), F32).at[:n_lat, lane0:lane0 + d_rot].set(sin)
    return cos_full, sin_full


def _pad_heads(w, n_heads, width, lane0=0):
    lead = w.shape[:-1]
    w = w.reshape(*lead, n_heads, width)
    w = jnp.pad(w, [(0, 0)] * len(lead) + [(0, 0), (lane0, HEAD_LANES - lane0 - width)])
    return w.reshape(*lead, n_heads * HEAD_LANES)


def _pad_lanes(v, lane0=0):
    return jnp.pad(v, (lane0, HEAD_LANES - lane0 - v.shape[0])).reshape(1, HEAD_LANES)


def kernel(x, c, ctx, c_ctx, w_ada, b_ada, norm1_g, norm2_g, w_in_e, w_out_e, s5_a_re, s5_a_im, s5_log_dt, s5_b_re, s5_b_im, s5_c_re, s5_c_im, s5_d, s5_w_glu, mla_gq, mla_w_uq, mla_gkv, mla_w_ukv, mla_qn, mla_kn, w_qkv_o, w_out_o, gqa_qn, gqa_kn, moe_w_r1, moe_b_r1, moe_w_r2, moe_b_r2, moe_w_gate, moe_w_up, moe_w_down):
    batch, n_lat, _ = x.shape
    n_ctx = ctx.shape[1]
    depth = w_ada.shape[0]
    rows = _Rows(batch, n_lat, n_ctx, TM)
    n_lat_rows = batch * n_lat

    cond = jnp.zeros((8, D_MODEL), F32).at[:batch].set(c).at[batch].set(c_ctx)
    mods_all = _ada_table(cond, w_ada, b_ada).reshape(depth, 8, 1, N_MOD * D_MODEL)

    rope_gqa = _rope_tables(n_lat, GQA_DIM, 0)
    rope_mla = _rope_tables(n_lat, MLA_ROPE, MLA_NOPE)

    xs = jnp.concatenate([x.reshape(n_lat_rows, D_MODEL), ctx.reshape(batch * n_ctx, D_MODEL)], axis=0)

    for l in range(depth):
        last = l == depth - 1
        i = l // 2
        mods = mods_all[l]
        g1 = norm1_g[l].reshape(1, D_MODEL)
        g2 = norm2_g[l].reshape(1, D_MODEL)
        if l % 2 == 0:
            w_in = jnp.pad(w_in_e[i], ((0, 0), (0, D_MODEL - w_in_e.shape[2]))).astype(BF16)
            w_ukv = mla_w_ukv[i].reshape(MLA_KV_RANK, MLA_HEADS, MLA_NOPE + MLA_V)
            w_k = _pad_heads(w_ukv[..., :MLA_NOPE].reshape(MLA_KV_RANK, -1), MLA_HEADS, MLA_NOPE).astype(BF16)
            w_v = w_ukv[..., MLA_NOPE:].reshape(MLA_KV_RANK, -1).astype(BF16)
            u, q, k, vt = _in_even(
                xs, mods, g1, w_in, mla_gq[i].reshape(1, -1),
                _pad_heads(mla_w_uq[i], MLA_HEADS, MLA_QK).astype(BF16), mla_gkv[i].reshape(1, -1), w_k, w_v,
                _pad_lanes(mla_qn[i]), _pad_lanes(mla_kn[i]), *rope_mla, rows)
            o = _attention_all(q, k, vt, batch, n_lat, n_ctx, False, MLA_V, not last)
            s5p = _s5_params(s5_a_re[i], s5_a_im[i], s5_log_dt[i], s5_b_re[i], s5_b_im[i], s5_c_re[i], s5_c_im[i])
            y_f, y_r = _s5_mixer(u, s5p, batch, n_lat, n_ctx)
            mixer_inputs = (y_f, y_r, u, s5_d[i].reshape(1, -1), s5_w_glu[i].astype(BF16), o,
                            w_out_e[i][:S5_WIDTH].astype(BF16), w_out_e[i][S5_WIDTH:].astype(BF16))
        else:
            q, k, vt = _in_odd(xs, mods, g1, w_qkv_o[i].astype(BF16), gqa_qn[i].reshape(1, -1),
                               gqa_kn[i].reshape(1, -1), *rope_gqa, rows)
            o = _attention_all(q, k, vt, batch, n_lat, n_ctx, True, GQA_DIM, not last)
            mixer_inputs = (o, w_out_o[i].astype(BF16))

        w_route = jnp.pad(jnp.concatenate([moe_w_r1[l], moe_w_r2[l]], axis=1),
                          ((0, 0), (0, HEAD_LANES - MOE_GROUPS - MOE_EXPERTS)))
        b_route = _pad_lanes(jnp.concatenate([moe_b_r1[l], moe_b_r2[l]]))
        n_tiles = rows.lat_tiles if last else rows.all_tiles
        n_tok = n_tiles * TM
        xs, g_tiles, route, counts = _out_proj(xs, mods, g2, mixer_inputs, w_route, b_route, rows, n_tiles,
                                               l % 2 == 0)
        dest, blk_e, n_used, pad_lo, pad_hi, n_slots = _moe_plan(route, counts, n_tok)
        xs_tiles = _dispatch(g_tiles, dest, pad_lo, pad_hi, n_tok, n_slots)
        ys_tiles = _moe_ffn(xs_tiles, blk_e, n_used, moe_w_gate, moe_w_up, moe_w_down, l)
        xs = _combine(xs, mods, route, ys_tiles, dest, rows, n_tok)
    return xs[:n_lat_rows].reshape(batch, n_lat, D_MODEL)
```
